```python
import math
import jax, jax.numpy as jnp
from jax import lax
import numpy as np

D_MODEL = 1024
BATCH = 8
SEQ = 4096
DEPTH = 2
DEC_BATCH = 32
DEC_SEQ = 1
PAST_LEN = 16384
PAGE_SIZE = 128

HEAD_DIM = 64
NSA_HEADS = 8
NSA_KV_HEADS = 2
NSA_GROUP = NSA_HEADS // NSA_KV_HEADS
NSA_BLOCK = 64
NSA_TOPK = 16
NSA_WINDOW = 512
NSA_WIDTH = NSA_HEADS * HEAD_DIM
DIFF_HEADS = 4
DIFF_V_DIM = 2 * HEAD_DIM
DIFF_WIDTH = DIFF_HEADS * DIFF_V_DIM
D_FF = 2816
CONV_W = 3
Q_BLOCK = 128
RMS_EPS = 1e-6
ATTN_SCALE = HEAD_DIM ** -0.5
FORCE_SCORE = 1e4
IN_SIZES = (NSA_WIDTH, 2 * NSA_KV_HEADS * HEAD_DIM, 2 * NSA_KV_HEADS * HEAD_DIM, 2 * NSA_KV_HEADS * HEAD_DIM,
            3 * NSA_HEADS, 2 * DIFF_HEADS * HEAD_DIM, 2 * DIFF_HEADS * HEAD_DIM, DIFF_WIDTH, 2 * D_MODEL)
N_IN = (NSA_WIDTH + 6 * NSA_KV_HEADS * HEAD_DIM + 3 * NSA_HEADS + 4 * DIFF_HEADS * HEAD_DIM + DIFF_WIDTH + 2 * D_MODEL)

kernel_name = 'nsa_diffattn_convglu_hybrid_step'


def rms_norm(x, g):
    xf = x.astype(jnp.float32)
    y = xf * lax.rsqrt(jnp.mean(xf * xf, axis=-1, keepdims=True) + RMS_EPS)
    return (y * g.astype(jnp.float32)).astype(x.dtype)


def masked_softmax(s, mask):
    s = jnp.where(mask, s, -jnp.inf)
    m = jnp.max(s, axis=-1, keepdims=True)
    m = jnp.where(jnp.isfinite(m), m, 0.0)
    p = jnp.exp(s - m)
    return p / jnp.maximum(jnp.sum(p, axis=-1, keepdims=True), 1e-30)


def alibi_slopes(n):
    return 2.0 ** (-8.0 * jnp.arange(1, n + 1, dtype=jnp.float32) / n)


def project(xn, w_in, g_nsa, g_diff):
    B, T, _ = xn.shape
    offs = np.cumsum(IN_SIZES)[:-1].tolist()
    q, kvc, kvs, kvw, gn, qd, kd, vd, gm = jnp.split(xn @ w_in, offs, axis=-1)
    q = rms_norm(q.reshape(B, T, NSA_KV_HEADS, NSA_GROUP, HEAD_DIM), g_nsa[0])

    def nsa_kv(z, g):
        z = z.reshape(B, T, 2, NSA_KV_HEADS, HEAD_DIM)
        return jnp.stack([rms_norm(z[:, :, 0], g), z[:, :, 1]], axis=2)

    kvc = nsa_kv(kvc, g_nsa[1])
    kvs = nsa_kv(kvs, g_nsa[2])
    kvw = nsa_kv(kvw, g_nsa[3])
    gate = jax.nn.sigmoid(gn.reshape(B, T, 3, NSA_KV_HEADS, NSA_GROUP))
    qd = qd.reshape(B, T, 2, DIFF_HEADS, HEAD_DIM)
    kd = kd.reshape(B, T, 2, DIFF_HEADS, HEAD_DIM)
    qdiff = jnp.concatenate([rms_norm(qd[:, :, 0], g_diff[0]), rms_norm(qd[:, :, 1], g_diff[1])], axis=-1)
    kdiff = jnp.concatenate([rms_norm(kd[:, :, 0], g_diff[2]), rms_norm(kd[:, :, 1], g_diff[3])], axis=-1)
    kvd = jnp.stack([kdiff, vd.reshape(B, T, DIFF_HEADS, DIFF_V_DIM)], axis=2)
    gm = jax.nn.sigmoid(gm.reshape(B, T, 2, D_MODEL))
    return q, gate, kvc, kvs, kvw, qdiff, kvd, gm


def compress(k, w):
    B, Tk = k.shape[:2]
    nc = Tk // NSA_BLOCK
    kb = k[:, :nc * NSA_BLOCK].reshape(B, nc, NSA_BLOCK, NSA_KV_HEADS, HEAD_DIM)
    return jnp.einsum('bnlgd,l->bngd', kb, w)


def to_blocks(k):
    B, Tk = k.shape[:2]
    ns = -(-Tk // NSA_BLOCK)
    k = jnp.pad(k, ((0, 0), (0, ns * NSA_BLOCK - Tk), (0, 0), (0, 0)))
    return k.reshape(B, ns, NSA_BLOCK, NSA_KV_HEADS, HEAD_DIM).transpose(0, 3, 1, 2, 4)


def nsa_compressed(q, kc, vc, qpos, slopes):
    nc = kc.shape[1]
    bend = (jnp.arange(nc) + 1) * NSA_BLOCK - 1
    dist = qpos[:, None] - bend[None, :]
    s = jnp.einsum('btgrd,bngd->bgrtn', q, kc).astype(jnp.float32) * ATTN_SCALE
    s = s - slopes[:, :, None, None] * dist.astype(jnp.float32)
    p = masked_softmax(s, dist >= 0)
    o = jnp.einsum('bgrtn,bngd->btgrd', p.astype(vc.dtype), vc)
    return o, jnp.sum(p, axis=2)


def nsa_selected(q, ksb, vsb, qpos, imp, slopes):
    B, _, ns = ksb.shape[:3]
    Tq = q.shape[1]
    score = jnp.pad(imp, ((0, 0), (0, 0), (0, 0), (0, ns - imp.shape[-1])))
    blk = jnp.arange(ns)[None, :]
    cur = (qpos // NSA_BLOCK)[:, None]
    forced = (blk == 0) | (blk == cur) | (blk == cur - 1)
    score = jnp.where(forced, FORCE_SCORE, jnp.where(blk > cur, -1.0, score))
    n_sel = min(NSA_TOPK, ns)
    _, idx = lax.top_k(score, n_sel)
    bi = jnp.arange(B)[:, None, None, None]
    gi = jnp.arange(NSA_KV_HEADS)[None, :, None, None]
    kg = ksb[bi, gi, idx]
    vg = vsb[bi, gi, idx].reshape(B, NSA_KV_HEADS, Tq, n_sel * NSA_BLOCK, HEAD_DIM)
    kpos = idx[..., None] * NSA_BLOCK + jnp.arange(NSA_BLOCK)
    dist = (qpos[None, None, :, None, None] - kpos).reshape(B, NSA_KV_HEADS, 1, Tq, n_sel * NSA_BLOCK)
    s = jnp.einsum('btgrd,bgtkld->bgrtkl', q, kg).astype(jnp.float32)
    s = s.reshape(B, NSA_KV_HEADS, NSA_GROUP, Tq, n_sel * NSA_BLOCK) * ATTN_SCALE
    s = s - slopes[None, :, :, None, None] * dist.astype(jnp.float32)
    p = masked_softmax(s, dist >= 0)
    return jnp.einsum('bgrtj,bgtjd->btgrd', p.astype(vg.dtype), vg)


def nsa_window(q, kw, vw, qpos, kpos, slopes):
    dist = qpos[:, None] - kpos[None, :]
    s = jnp.einsum('btgrd,bsgd->bgrts', q, kw).astype(jnp.float32) * ATTN_SCALE
    s = s - slopes[:, :, None, None] * dist.astype(jnp.float32)
    mask = (dist >= 0) & (dist < NSA_WINDOW) & (kpos >= 0)[None, :]
    p = masked_softmax(s, mask)
    return jnp.einsum('bgrts,bsgd->btgrd', p.astype(vw.dtype), vw)


def nsa_core(q, gate, qpos, kc, vc, ksb, vsb, kw, vw, kwpos, slopes):
    B, Tq = q.shape[:2]
    o_c, imp = nsa_compressed(q, kc, vc, qpos, slopes)
    o_s = nsa_selected(q, ksb, vsb, qpos, imp, slopes)
    o_w = nsa_window(q, kw, vw, qpos, kwpos, slopes)
    o = (gate[:, :, 0, :, :, None] * o_c + gate[:, :, 1, :, :, None] * o_s
         + gate[:, :, 2, :, :, None] * o_w)
    return o.reshape(B, Tq, NSA_WIDTH)


def nsa_prompt(q, gate, kvc, kvs, kvw, cmp_w, slopes):
    B, S = q.shape[:2]
    nb = S // Q_BLOCK
    kc = compress(kvc[:, :, 0], cmp_w[0])
    vc = compress(kvc[:, :, 1], cmp_w[1])
    ksb = to_blocks(kvs[:, :, 0])
    vsb = to_blocks(kvs[:, :, 1])
    kvw_pad = jnp.pad(kvw, ((0, 0), (NSA_WINDOW, 0), (0, 0), (0, 0), (0, 0)))
    kwpos_all = jnp.arange(S + NSA_WINDOW) - NSA_WINDOW
    qb = q.reshape(B, nb, Q_BLOCK, NSA_KV_HEADS, NSA_GROUP, HEAD_DIM).swapaxes(0, 1)
    gb = gate.reshape(B, nb, Q_BLOCK, 3, NSA_KV_HEADS, NSA_GROUP).swapaxes(0, 1)

    def block(args):
        i, qi, gi = args
        start = i * Q_BLOCK
        qpos = start + jnp.arange(Q_BLOCK)
        kw = lax.dynamic_slice_in_dim(kvw_pad, start, NSA_WINDOW + Q_BLOCK, axis=1)
        kwpos = lax.dynamic_slice_in_dim(kwpos_all, start, NSA_WINDOW + Q_BLOCK)
        return nsa_core(qi, gi, qpos, kc, vc, ksb, vsb, kw[:, :, 0], kw[:, :, 1], kwpos, slopes)

    o = lax.map(block, (jnp.arange(nb), qb, gb))
    return o.swapaxes(0, 1).reshape(B, S, NSA_WIDTH)


def lambda_value(lp, lam_init):
    lp = lp.astype(jnp.float32)
    return jnp.exp(jnp.sum(lp[0] * lp[1])) - jnp.exp(jnp.sum(lp[2] * lp[3])) + lam_init


def diff_attend(q, k, v, qpos, kpos, slopes, lam):
    B, Tq, H, _ = q.shape
    Tk = k.shape[1]
    qm = q.reshape(B, Tq, H, 2, HEAD_DIM)
    km = k.reshape(B, Tk, H, 2, HEAD_DIM)
    s = jnp.einsum('bthmd,bshmd->bmhts', qm, km).astype(jnp.float32) * ATTN_SCALE
    dist = qpos[:, None] - kpos[None, :]
    s = s - slopes[:, None, None] * dist.astype(jnp.float32)
    p = masked_softmax(s, dist >= 0)
    a = p[:, 0] - lam * p[:, 1]
    return jnp.einsum('bhts,bshe->bthe', a.astype(v.dtype), v)


def diff_prompt(qd, kvd, slopes, lam):
    B, S = qd.shape[:2]
    nb = S // Q_BLOCK
    k, v = kvd[:, :, 0], kvd[:, :, 1]
    kpos = jnp.arange(S)
    qb = qd.reshape(B, nb, Q_BLOCK, DIFF_HEADS, 2 * HEAD_DIM).swapaxes(0, 1)

    def block(args):
        i, qi = args
        return diff_attend(qi, k, v, i * Q_BLOCK + jnp.arange(Q_BLOCK), kpos, slopes, lam)

    o = lax.map(block, (jnp.arange(nb), qb))
    return o.swapaxes(0, 1).reshape(B, S, DIFF_HEADS, DIFF_V_DIM)


def diff_out(o, subln, lam_init):
    B, T = o.shape[:2]
    return (rms_norm(o, subln) * (1.0 - lam_init)).reshape(B, T, DIFF_WIDTH)


def merge_branches(o_a, o_b, gm, w_br_a, w_br_b, w_o):
    y = gm[:, :, 0] * (o_a @ w_br_a) + gm[:, :, 1] * (o_b @ w_br_b)
    return y @ w_o


def conv_ffn(xn, buf, w_up, conv_w, conv_b, w_down):
    u = xn @ w_up
    T = u.shape[1]
    ext = jnp.concatenate([buf.astype(u.dtype), u], axis=1)
    c = conv_b + conv_w[CONV_W - 1] * u
    for j in range(CONV_W - 1):
        c = c + conv_w[j] * ext[:, j:j + T]
    a, g = jnp.split(c, 2, axis=-1)
    return (jax.nn.silu(g) * a) @ w_down, ext[:, -(CONV_W - 1):]


def gather_pages(pool, page_table, layer):
    g = pool[page_table, layer]
    return g.reshape(g.shape[0], g.shape[1] * g.shape[2], *g.shape[3:])


def setup_inputs(seed: int = 0) -> dict:
    key = jax.random.key(seed)
    ks = jax.random.split(key, 24)
    n_pages = PAST_LEN // PAGE_SIZE
    n_used = DEC_BATCH * n_pages
    n_phys = n_used + max(1, n_used // 4)
    w_buf = min(NSA_WINDOW, PAST_LEN)
    f32 = jnp.float32

    def nrm(k, shape, scale=1.0):
        return jax.random.normal(k, shape, f32) * scale

    def gain(k, shape):
        return 1.0 + nrm(k, shape, 0.02)

    page_table = jax.random.permutation(ks[0], n_phys)[:n_used].reshape(DEC_BATCH, n_pages).astype(jnp.int32)
    return {
        'x_prompt': nrm(ks[1], (BATCH, SEQ, D_MODEL)),
        'x_sample': nrm(ks[2], (DEC_BATCH, DEC_SEQ, D_MODEL)),
        'cache_nsa_cmp': nrm(ks[3], (n_phys, DEPTH, PAGE_SIZE, 2, NSA_KV_HEADS, HEAD_DIM)),
        'cache_nsa_slc': nrm(ks[4], (n_phys, DEPTH, PAGE_SIZE, 2, NSA_KV_HEADS, HEAD_DIM)),
        'cache_diff': nrm(ks[5], (n_phys, DEPTH, PAGE_SIZE, 2, DIFF_HEADS, DIFF_V_DIM)),
        'state_nsa_win': nrm(ks[6], (DEPTH, DEC_BATCH, w_buf, 2, NSA_KV_HEADS, HEAD_DIM)),
        'state_ffn_conv': nrm(ks[7], (DEPTH, DEC_BATCH, CONV_W - 1, 2 * D_FF)),
        'page_table': page_table,
        'norm_attn': gain(ks[8], (DEPTH, D_MODEL)),
        'w_in': nrm(ks[9], (DEPTH, D_MODEL, N_IN), D_MODEL ** -0.5),
        'qk_gain_nsa': gain(ks[10], (DEPTH, 4, HEAD_DIM)),
        'qk_gain_diff': gain(ks[11], (DEPTH, 4, HEAD_DIM)),
        'nsa_cmp_w': (1.0 + nrm(ks[12], (DEPTH, 2, NSA_BLOCK), 0.1)) * NSA_BLOCK ** -0.5,
        'diff_lambda': nrm(ks[13], (DEPTH, 4, HEAD_DIM), 0.1),
        'diff_subln': gain(ks[14], (DEPTH, DIFF_V_DIM)),
        'w_br_a': nrm(ks[15], (DEPTH, NSA_WIDTH, D_MODEL), NSA_WIDTH ** -0.5),
        'w_br_b': nrm(ks[16], (DEPTH, DIFF_WIDTH, D_MODEL), DIFF_WIDTH ** -0.5),
        'w_o': nrm(ks[17], (DEPTH, D_MODEL, D_MODEL), D_MODEL ** -0.5),
        'norm_ffn': gain(ks[18], (DEPTH, D_MODEL)),
        'w_up': nrm(ks[19], (DEPTH, D_MODEL, 2 * D_FF), D_MODEL ** -0.5),
        'conv_w': nrm(ks[20], (DEPTH, CONV_W, 2 * D_FF), CONV_W ** -0.5),
        'conv_b': nrm(ks[21], (DEPTH, 2 * D_FF), 0.01),
        'w_down': nrm(ks[22], (DEPTH, D_FF, D_MODEL), D_FF ** -0.5),
    }


def reference(x_prompt, x_sample, cache_nsa_cmp, cache_nsa_slc, cache_diff, state_nsa_win, state_ffn_conv,
              page_table, norm_attn, w_in, qk_gain_nsa, qk_gain_diff, nsa_cmp_w, diff_lambda, diff_subln,
              w_br_a, w_br_b, w_o, norm_ffn, w_up, conv_w, conv_b, w_down):
    past = page_table.shape[1] * PAGE_SIZE
    t_new = x_sample.shape[1]
    w_buf = state_nsa_win.shape[2]
    w_keep_p = min(NSA_WINDOW, x_prompt.shape[1])
    w_keep_s = min(NSA_WINDOW, w_buf + t_new)
    qpos_s = past + jnp.arange(t_new)
    kpos_s = jnp.arange(past + t_new)
    kwpos_s = jnp.arange(past - w_buf, past + t_new)
    slopes_nsa = alibi_slopes(NSA_HEADS).reshape(NSA_GROUP, NSA_KV_HEADS).T
    slopes_diff = alibi_slopes(DIFF_HEADS)

    xp, xs = x_prompt, x_sample
    p_cmp, p_slc, p_diff, p_win, p_conv = [], [], [], [], []
    s_cmp, s_slc, s_diff, s_win, s_conv = [], [], [], [], []
    for l in range(DEPTH):
        lam_init = 0.8 - 0.6 * math.exp(-0.3 * l)
        lam = lambda_value(diff_lambda[l], lam_init)

        xn = rms_norm(xp, norm_attn[l])
        q, gate, kvc, kvs, kvw, qd, kvd, gm = project(xn, w_in[l], qk_gain_nsa[l], qk_gain_diff[l])
        o_a = nsa_prompt(q, gate, kvc, kvs, kvw, nsa_cmp_w[l], slopes_nsa)
        o_b = diff_out(diff_prompt(qd, kvd, slopes_diff, lam), diff_subln[l], lam_init)
        xp = xp + merge_branches(o_a, o_b, gm, w_br_a[l], w_br_b[l], w_o[l])
        zero_buf = jnp.zeros((xp.shape[0], CONV_W - 1, 2 * D_FF), xp.dtype)
        f, conv_p = conv_ffn(rms_norm(xp, norm_ffn[l]), zero_buf, w_up[l], conv_w[l], conv_b[l], w_down[l])
        xp = xp + f
        p_cmp.append(kvc)
        p_slc.append(kvs)
        p_diff.append(kvd)
        p_win.append(kvw[:, -w_keep_p:])
        p_conv.append(conv_p)

        xn = rms_norm(xs, norm_attn[l])
        q, gate, kvc, kvs, kvw, qd, kvd, gm = project(xn, w_in[l], qk_gain_nsa[l], qk_gain_diff[l])
        kvc_all = jnp.concatenate([gather_pages(cache_nsa_cmp, page_table, l).astype(kvc.dtype), kvc], axis=1)
        kvs_all = jnp.concatenate([gather_pages(cache_nsa_slc, page_table, l).astype(kvs.dtype), kvs], axis=1)
        kvd_all = jnp.concatenate([gather_pages(cache_diff, page_table, l).astype(kvd.dtype), kvd], axis=1)
        kvw_all = jnp.concatenate([state_nsa_win[l].astype(kvw.dtype), kvw], axis=1)
        kc = compress(kvc_all[:, :, 0], nsa_cmp_w[l, 0])
        vc = compress(kvc_all[:, :, 1], nsa_cmp_w[l, 1])
        o_a = nsa_core(q, gate, qpos_s, kc, vc, to_blocks(kvs_all[:, :, 0]), to_blocks(kvs_all[:, :, 1]),
                       kvw_all[:, :, 0], kvw_all[:, :, 1], kwpos_s, slopes_nsa)
        o_b = diff_out(diff_attend(qd, kvd_all[:, :, 0], kvd_all[:, :, 1], qpos_s, kpos_s, slopes_diff, lam),
                       diff_subln[l], lam_init)
        xs = xs + merge_branches(o_a, o_b, gm, w_br_a[l], w_br_b[l], w_o[l])
        f, conv_s = conv_ffn(rms_norm(xs, norm_ffn[l]), state_ffn_conv[l], w_up[l], conv_w[l], conv_b[l], w_down[l])
        xs = xs + f
        s_cmp.append(kvc)
        s_slc.append(kvs)
        s_diff.append(kvd)
        s_win.append(kvw_all[:, -w_keep_s:])
        s_conv.append(conv_s)

    new_nsa_cmp_prompt = jnp.stack(p_cmp, axis=1)
    new_nsa_slc_prompt = jnp.stack(p_slc, axis=1)
    new_diff_prompt = jnp.stack(p_diff, axis=1)
    new_nsa_win_prompt = jnp.stack(p_win, axis=0)
    new_ffn_conv_prompt = jnp.stack(p_conv, axis=0)
    new_nsa_cmp_sample = jnp.stack(s_cmp, axis=1)
    new_nsa_slc_sample = jnp.stack(s_slc, axis=1)
    new_diff_sample = jnp.stack(s_diff, axis=1)
    new_nsa_win_sample = jnp.stack(s_win, axis=0)
    new_ffn_conv_sample = jnp.stack(s_conv, axis=0)
    return (xp, xs, new_nsa_cmp_prompt, new_nsa_slc_prompt, new_diff_prompt, new_nsa_win_prompt,
            new_ffn_conv_prompt, new_nsa_cmp_sample, new_nsa_slc_sample, new_diff_sample,
            new_nsa_win_sample, new_ffn_conv_sample)
```

```python
import functools
import math

import numpy as np
import jax
import jax.numpy as jnp
from jax import lax
from jax.experimental import pallas as pl
from jax.experimental.pallas import tpu as pltpu

HEAD_DIM = 64
NSA_HEADS = 8
NSA_KV_HEADS = 2
NSA_GROUP = NSA_HEADS // NSA_KV_HEADS
NSA_BLOCK = 64
NSA_TOPK = 16
NSA_WINDOW = 512
NSA_WIDTH = NSA_HEADS * HEAD_DIM
DIFF_HEADS = 4
DIFF_V_DIM = 2 * HEAD_DIM
DIFF_WIDTH = DIFF_HEADS * DIFF_V_DIM
CONV_W = 3
PAGE_SIZE = 128
RMS_EPS = 1e-6
ATTN_SCALE = HEAD_DIM ** -0.5
FORCE_SCORE = 1e4
KV_W = 2 * NSA_KV_HEADS * HEAD_DIM
LANES = 128
GATE_PAD = LANES
NEG = -1e30
VMEM_LIMIT = 56 * 1024 * 1024

F32 = jnp.float32
BF16 = jnp.bfloat16
HIGHEST = lax.Precision.HIGHEST

C_Q = 0
C_KVC = C_Q + NSA_WIDTH
C_KVS = C_KVC + KV_W
C_KVW = C_KVS + KV_W
C_GATE = C_KVW + KV_W
C_QD = C_GATE + GATE_PAD
C_KD = C_QD + DIFF_WIDTH
C_VD = C_KD + DIFF_WIDTH
C_GM = C_VD + DIFF_WIDTH


def _nsa_slope(h):
    g, r = h // NSA_GROUP, h % NSA_GROUP
    return 2.0 ** (-8.0 * (r * NSA_KV_HEADS + g + 1) / NSA_HEADS)


def _diff_slope(h):
    return 2.0 ** (-8.0 * (h + 1) / DIFF_HEADS)


def _nt(a, b, **kw):
    return lax.dot_general(a, b, (((1,), (1,)), ((), ())), preferred_element_type=F32, **kw)


def _nn(a, b, **kw):
    return lax.dot_general(a, b, (((1,), (0,)), ((), ())), preferred_element_type=F32, **kw)


def _cparams(sem):
    return pltpu.CompilerParams(dimension_semantics=sem, vmem_limit_bytes=VMEM_LIMIT)


def _proj_kernel(x_ref, gn_ref, w_ref, gains_ref, pmat_ref,
                 q_o, kvc_o, kvs_o, kvw_o, gate_o, qd_o, kvd_o, gm_o):
    x = x_ref[...]
    ms = jnp.mean(x * x, axis=-1, keepdims=True)
    xn = (x * lax.rsqrt(ms + RMS_EPS) * gn_ref[...]).astype(BF16)
    pmat = pmat_ref[...]

    def seg(c0, width):
        return jnp.dot(xn, w_ref[:, c0:c0 + width], preferred_element_type=F32)

    def headnorm(z, gain_row):
        zz = z * z
        hi = zz.astype(BF16)
        lo = (zz - hi.astype(F32)).astype(BF16)
        msq = jnp.dot(hi, pmat, preferred_element_type=F32) + jnp.dot(lo, pmat, preferred_element_type=F32)
        return z * lax.rsqrt(msq + RMS_EPS) * gains_ref[gain_row:gain_row + 1, :]

    def normed(c0, width, gain_row):
        z = seg(c0, width)
        return jnp.concatenate([headnorm(z[:, c:c + LANES], gain_row) for c in range(0, width, LANES)], axis=1)

    q_o[...] = normed(C_Q, NSA_WIDTH, 0)
    for c0, out, row in ((C_KVC, kvc_o, 1), (C_KVS, kvs_o, 2), (C_KVW, kvw_o, 3)):
        z = seg(c0, KV_W)
        out[...] = jnp.concatenate([headnorm(z[:, :LANES], row), z[:, LANES:]], axis=1)
    gate_o[...] = jax.nn.sigmoid(seg(C_GATE, GATE_PAD))
    qd_o[...] = normed(C_QD, DIFF_WIDTH, 4)
    kvd_o[:, :DIFF_WIDTH] = normed(C_KD, DIFF_WIDTH, 5)
    kvd_o[:, DIFF_WIDTH:] = seg(C_VD, DIFF_WIDTH)
    d_model = x.shape[-1]
    for c in range(0, 2 * d_model, 512):
        gm_o[:, c:c + 512] = jax.nn.sigmoid(seg(C_GM + c, 512))


def _proj(x2d, gn, w_packed, gains, pmat, tm):
    T, D = x2d.shape
    widths = (NSA_WIDTH, KV_W, KV_W, KV_W, GATE_PAD, DIFF_WIDTH, 2 * DIFF_WIDTH, 2 * D)
    full = lambda a: pl.BlockSpec(a.shape, lambda i: (0, 0))
    return pl.pallas_call(
        _proj_kernel,
        grid=(T // tm,),
        in_specs=[pl.BlockSpec((tm, D), lambda i: (i, 0)), full(gn), full(w_packed), full(gains), full(pmat)],
        out_specs=[pl.BlockSpec((tm, w), lambda i: (i, 0)) for w in widths],
        out_shape=[jax.ShapeDtypeStruct((T, w), F32) for w in widths],
        compiler_params=_cparams(("parallel",)),
    )(x2d, gn, w_packed, gains, pmat)


def _pack_w_in(w_in_l):
    D = w_in_l.shape[0]
    sizes = (NSA_WIDTH, KV_W, KV_W, KV_W, 3 * NSA_HEADS, DIFF_WIDTH, DIFF_WIDTH, DIFF_WIDTH, 2 * D)
    offs = np.cumsum(sizes)[:-1].tolist()
    q, kvc, kvs, kvw, gn, qd, kd, vd, gm = jnp.split(w_in_l, offs, axis=1)
    gn = jnp.pad(gn, ((0, 0), (0, GATE_PAD - gn.shape[1])))
    perm = lambda w: w.reshape(D, 2, DIFF_HEADS, HEAD_DIM).transpose(0, 2, 1, 3).reshape(D, DIFF_WIDTH)
    return jnp.concatenate([q, kvc, kvs, kvw, gn, perm(qd), perm(kd), vd, gm], axis=1).astype(BF16)


def _pack_gains(g_nsa, g_diff):
    two = lambda a, b: jnp.concatenate([a, b])[None, :]
    rows = [two(g_nsa[0], g_nsa[0]), two(g_nsa[1], g_nsa[1]), two(g_nsa[2], g_nsa[2]), two(g_nsa[3], g_nsa[3]),
            two(g_diff[0], g_diff[1]), two(g_diff[2], g_diff[3])]
    rows += [jnp.ones((1, LANES), F32)] * 2
    return jnp.concatenate(rows, axis=0).astype(F32)


def _avg_matrix():
    i = np.arange(LANES)
    return jnp.asarray((i[:, None] // HEAD_DIM == i[None, :] // HEAD_DIM) / HEAD_DIM, dtype=BF16)


def _compress_kernel(*refs, n_in, n_prefetch):
    refs = refs[n_prefetch:]
    w_ref, out_ref = refs[n_in], refs[n_in + 1]
    w = w_ref[...]
    outs = []
    for r in refs[:n_in]:
        x = r[...].reshape(-1, NSA_BLOCK, KV_W)
        outs.append(jnp.sum(x * w[None], axis=1))
    out_ref[...] = jnp.concatenate(outs, axis=0).reshape(out_ref.shape)


def _cmp_weights(cmp_w_l):
    return jnp.concatenate([jnp.broadcast_to(cmp_w_l[0][:, None], (NSA_BLOCK, KV_W // 2)),
                            jnp.broadcast_to(cmp_w_l[1][:, None], (NSA_BLOCK, KV_W // 2))], axis=1).astype(F32)


def _compress_prompt(kvc, wexp, tk):
    B, S, _ = kvc.shape
    nb = tk // NSA_BLOCK
    return pl.pallas_call(
        functools.partial(_compress_kernel, n_in=1, n_prefetch=0),
        grid=(B, S // tk),
        in_specs=[pl.BlockSpec((None, tk, KV_W), lambda b, i: (b, i, 0)),
                  pl.BlockSpec(wexp.shape, lambda b, i: (0, 0))],
        out_specs=pl.BlockSpec((None, nb, KV_W), lambda b, i: (b, i, 0)),
        out_shape=jax.ShapeDtypeStruct((B, S // NSA_BLOCK, KV_W), F32),
        compiler_params=_cparams(("parallel", "parallel")),
    )(kvc, wexp)


def _page_specs(n, layer, width):
    def mk(i):
        return pl.BlockSpec((None, None, PAGE_SIZE, width), lambda b, c, pt: (pt[b, c * n + i], layer, 0, 0))
    return [mk(i) for i in range(n)]


def _compress_pages(pool, page_table, wexp, layer, ppc):
    DB, n_pages = page_table.shape
    bpp = PAGE_SIZE // NSA_BLOCK
    grid_spec = pltpu.PrefetchScalarGridSpec(
        num_scalar_prefetch=1,
        grid=(DB, n_pages // ppc),
        in_specs=_page_specs(ppc, layer, KV_W) + [pl.BlockSpec(wexp.shape, lambda b, c, pt: (0, 0))],
        out_specs=pl.BlockSpec((None, ppc * bpp, KV_W), lambda b, c, pt: (b, c, 0)),
    )
    return pl.pallas_call(
        functools.partial(_compress_kernel, n_in=ppc, n_prefetch=1),
        grid_spec=grid_spec,
        out_shape=jax.ShapeDtypeStruct((DB, n_pages * bpp, KV_W), F32),
        compiler_params=_cparams(("parallel", "parallel")),
    )(page_table, *([pool] * ppc), wexp)


def _to_rows(q, scale):
    t = q.shape[0]
    hi_half = lax.broadcasted_iota(jnp.int32, (t, LANES), 1) >= HEAD_DIM
    pieces = []
    for h in range(NSA_HEADS):
        g = h // NSA_GROUP
        c = q[:, (h // 2) * LANES:(h // 2 + 1) * LANES]
        if (h % 2) != g:
            c = pltpu.roll(c, HEAD_DIM, 1)
        keep = hi_half if g == 1 else jnp.logical_not(hi_half)
        pieces.append(jnp.where(keep, c * scale, 0.0))
    return jnp.concatenate(pieces, axis=0)


def _from_rows(o, t):
    lo_half = lax.broadcasted_iota(jnp.int32, (t, LANES), 1) < HEAD_DIM
    chunks = []
    for j in range(NSA_HEADS // 2):
        a = o[(2 * j) * t:(2 * j + 1) * t]
        b = o[(2 * j + 1) * t:(2 * j + 2) * t]
        if j // 2 == 0:
            chunks.append(jnp.where(lo_half, a, pltpu.roll(b, HEAD_DIM, 1)))
        else:
            chunks.append(jnp.where(lo_half, pltpu.roll(a, HEAD_DIM, 1), b))
    return jnp.concatenate(chunks, axis=1)


def _gate_rows(gate, branch, t):
    cols = [jnp.broadcast_to(gate[:, branch * NSA_HEADS + h:branch * NSA_HEADS + h + 1], (t, LANES))
            for h in range(NSA_HEADS)]
    return jnp.concatenate(cols, axis=0)


def _softmax_rows(s, valid):
    m = jnp.max(jnp.where(valid, s, NEG), axis=-1, keepdims=True)
    p = jnp.where(valid, jnp.exp(s - m), 0.0)
    return p / jnp.maximum(jnp.sum(p, axis=-1, keepdims=True), 1e-30)


def _nsa_prompt_kernel(qi_tab, ki_tab, last_tab, q_ref, gate_ref, cmp_ref, kvw_ref, kvs_ref, o_ref,
                       q2_s, sel_s, loc_s, g1_s, m_s, l_s, acc_s, *, tq, tk, seq):
    step = pl.program_id(1)
    qi, ki = qi_tab[step], ki_tab[step]
    q0 = qi * tq
    nc = seq // NSA_BLOCK
    ns = -(-seq // NSA_BLOCK)
    n_sel = min(NSA_TOPK, ns)
    wk = NSA_WINDOW + tq

    @pl.when(ki == 0)
    def _init():
        q2 = _to_rows(q_ref[...], ATTN_SCALE)
        q2b = q2.astype(BF16)
        q2_s[...] = q2b
        gate = gate_ref[...]
        qpos_col = q0 + lax.broadcasted_iota(jnp.int32, (tq, 1), 0)
        qpos_row = q0 + lax.broadcasted_iota(jnp.int32, (1, tq), 1)

        cmp = cmp_ref[...]
        kc, vc = cmp[:, :LANES], cmp[:, LANES:]
        bend_row = lax.broadcasted_iota(jnp.int32, (1, nc), 1) * NSA_BLOCK + (NSA_BLOCK - 1)
        dist_c = qpos_col - bend_row
        valid_c = dist_c >= 0
        s_c = _nt(q2, kc, precision=HIGHEST)
        o_c = []
        for h in range(NSA_HEADS):
            sh = s_c[h * tq:(h + 1) * tq] - _nsa_slope(h) * dist_c.astype(F32)
            o_c.append(_nn(_softmax_rows(sh, valid_c), vc, precision=HIGHEST))
        o_c = jnp.concatenate(o_c, axis=0)

        bend_col = lax.broadcasted_iota(jnp.int32, (nc, 1), 0) * NSA_BLOCK + (NSA_BLOCK - 1)
        dist_t = qpos_row - bend_col
        valid_t = dist_t >= 0
        s_t = _nt(kc, q2, precision=HIGHEST)
        blk = lax.broadcasted_iota(jnp.int32, (ns, tq), 0)
        cur = qpos_row // NSA_BLOCK
        sel_t = []
        for g in range(NSA_KV_HEADS):
            imp = jnp.zeros((nc, tq), F32)
            for r in range(NSA_GROUP):
                h = g * NSA_GROUP + r
                sh = s_t[:, h * tq:(h + 1) * tq] - _nsa_slope(h) * dist_t.astype(F32)
                m = jnp.max(jnp.where(valid_t, sh, NEG), axis=0, keepdims=True)
                p = jnp.where(valid_t, jnp.exp(sh - m), 0.0)
                imp = imp + p / jnp.maximum(jnp.sum(p, axis=0, keepdims=True), 1e-30)
            if ns > nc:
                imp = jnp.concatenate([imp, jnp.zeros((ns - nc, tq), F32)], axis=0)
            forced = (blk == 0) | (blk == cur) | (blk == cur - 1)
            score = jnp.where(forced, FORCE_SCORE, jnp.where(blk > cur, -1.0, imp))
            cnt = jnp.zeros((ns, tq), jnp.int32)
            for i in range(ns):
                row = score[i:i + 1, :]
                ahead = (row > score) | ((row == score) & (blk > i))
                cnt = cnt + ahead.astype(jnp.int32)
            sel_t.append((cnt < n_sel).astype(F32))
        sel_s[...] = jnp.concatenate(sel_t, axis=0).T

        start = pl.multiple_of(jnp.clip(q0 - NSA_WINDOW, 0, seq - wk), LANES)
        kvw = kvw_ref[pl.ds(start, wk), :]
        kw, vw = kvw[:, :LANES].astype(BF16), kvw[:, LANES:].astype(BF16)
        kpos = start + lax.broadcasted_iota(jnp.int32, (1, wk), 1)
        dist_w = qpos_col - kpos
        valid_w = (dist_w >= 0) & (dist_w < NSA_WINDOW)
        krel = (kpos - q0).astype(F32)
        s_w = _nt(q2b, kw)
        o_w = []
        for h in range(NSA_HEADS):
            sh = s_w[h * tq:(h + 1) * tq] + _nsa_slope(h) * krel
            o_w.append(_nn(_softmax_rows(sh, valid_w).astype(BF16), vw))
        o_w = jnp.concatenate(o_w, axis=0)

        loc_s[...] = _gate_rows(gate, 0, tq) * o_c + _gate_rows(gate, 2, tq) * o_w
        g1_s[...] = _gate_rows(gate, 1, tq)
        m_s[...] = jnp.full(m_s.shape, NEG, F32)
        l_s[...] = jnp.zeros(l_s.shape, F32)
        acc_s[...] = jnp.zeros(acc_s.shape, F32)

    kvs = kvs_ref[...]
    ks, vs = kvs[:, :LANES].astype(BF16), kvs[:, LANES:].astype(BF16)
    k0 = ki * tk
    kpos = k0 + lax.broadcasted_iota(jnp.int32, (1, tk), 1)
    qpos_col = q0 + lax.broadcasted_iota(jnp.int32, (tq, 1), 0)
    causal = kpos <= qpos_col
    krel = (kpos - q0).astype(F32)
    sel = sel_s[...].astype(BF16)
    kblk = kpos // NSA_BLOCK
    crow = lax.broadcasted_iota(jnp.int32, (2 * ns, tk), 0)
    valid = []
    for g in range(NSA_KV_HEADS):
        expand = (crow == kblk + g * ns).astype(BF16)
        valid.append((jnp.dot(sel, expand, preferred_element_type=F32) > 0.5) & causal)
    for h in range(NSA_HEADS):
        rows = slice(h * tq, (h + 1) * tq)
        v = valid[h // NSA_GROUP]
        s = _nt(q2_s[rows, :], ks) + _nsa_slope(h) * krel
        s = jnp.where(v, s, NEG)
        m_old = m_s[rows, :]
        m_new = jnp.maximum(m_old, jnp.max(s, axis=-1, keepdims=True))
        alpha = jnp.exp(m_old - m_new)
        p = jnp.where(v, jnp.exp(s - m_new), 0.0)
        l_s[rows, :] = alpha * l_s[rows, :] + jnp.sum(p, axis=-1, keepdims=True)
        acc_s[rows, :] = alpha * acc_s[rows, :] + _nn(p.astype(BF16), vs)
        m_s[rows, :] = m_new

    @pl.when(last_tab[step] == 1)
    def _fin():
        o_s = acc_s[...] / jnp.maximum(l_s[...], 1e-30)
        o_ref[...] = _from_rows(loc_s[...] + g1_s[...] * o_s, tq)


def _causal_pairs(seq, tq, tk):
    qi, ki, last = [], [], []
    for i in range(seq // tq):
        n = (i * tq + tq - 1) // tk + 1
        for j in range(n):
            qi.append(i), ki.append(j), last.append(1 if j == n - 1 else 0)
    mk = lambda a: jnp.asarray(np.asarray(a, dtype=np.int32))
    return mk(qi), mk(ki), mk(last)


def _nsa_prompt(q, gate, kvcmp, kvw, kvs, tq, tk):
    B, S, _ = q.shape
    nc = S // NSA_BLOCK
    ns = -(-S // NSA_BLOCK)
    qi, ki, last = _causal_pairs(S, tq, tk)
    rows = NSA_HEADS * tq
    grid_spec = pltpu.PrefetchScalarGridSpec(
        num_scalar_prefetch=3,
        grid=(B, int(qi.shape[0])),
        in_specs=[
            pl.BlockSpec((None, tq, NSA_WIDTH), lambda b, s, qt, kt, lt: (b, qt[s], 0)),
            pl.BlockSpec((None, tq, GATE_PAD), lambda b, s, qt, kt, lt: (b, qt[s], 0)),
            pl.BlockSpec((None, nc, KV_W), lambda b, s, qt, kt, lt: (b, 0, 0)),
            pl.BlockSpec((None, S, KV_W), lambda b, s, qt, kt, lt: (b, 0, 0)),
            pl.BlockSpec((None, tk, KV_W), lambda b, s, qt, kt, lt: (b, kt[s], 0)),
        ],
        out_specs=pl.BlockSpec((None, tq, NSA_WIDTH), lambda b, s, qt, kt, lt: (b, qt[s], 0)),
        scratch_shapes=[
            pltpu.VMEM((rows, LANES), BF16),
            pltpu.VMEM((tq, 2 * ns), F32),
            pltpu.VMEM((rows, LANES), F32),
            pltpu.VMEM((rows, LANES), F32),
            pltpu.VMEM((rows, 1), F32),
            pltpu.VMEM((rows, 1), F32),
            pltpu.VMEM((rows, LANES), F32),
        ],
    )
    return pl.pallas_call(
        functools.partial(_nsa_prompt_kernel, tq=tq, tk=tk, seq=S),
        grid_spec=grid_spec,
        out_shape=jax.ShapeDtypeStruct((B, S, NSA_WIDTH), F32),
        compiler_params=_cparams(("parallel", "arbitrary")),
    )(qi, ki, last, q, gate, kvcmp, kvw, kvs)


def _lambda_value(lp, lam_init):
    a = jnp.sum(lp[0:1] * lp[1:2], axis=-1, keepdims=True)
    b = jnp.sum(lp[2:3] * lp[3:4], axis=-1, keepdims=True)
    return jnp.exp(a) - jnp.exp(b) + lam_init


def _subln(o, g, lam_init):
    ms = jnp.mean(o * o, axis=-1, keepdims=True)
    return o * lax.rsqrt(ms + RMS_EPS) * g * (1.0 - lam_init)


def _diff_prompt_kernel(qi_tab, ki_tab, last_tab, q_ref, k_ref, v_ref, lam_ref, subln_ref, o_ref,
                        q2_s, m_s, l_s, acc_s, *, tq, tk, lam_init):
    h = pl.program_id(1)
    step = pl.program_id(2)
    qi, ki = qi_tab[step], ki_tab[step]
    q0 = qi * tq
    slope = jnp.float32(_diff_slope(DIFF_HEADS - 1))
    for i in range(DIFF_HEADS - 1):
        slope = jnp.where(h == i, jnp.float32(_diff_slope(i)), slope)

    @pl.when(ki == 0)
    def _init():
        q = q_ref[...] * ATTN_SCALE
        lo_half = lax.broadcasted_iota(jnp.int32, (tq, LANES), 1) < HEAD_DIM
        q2_s[...] = jnp.concatenate([jnp.where(lo_half, q, 0.0), jnp.where(lo_half, 0.0, q)], axis=0).astype(BF16)
        m_s[...] = jnp.full(m_s.shape, NEG, F32)
        l_s[...] = jnp.zeros(l_s.shape, F32)
        acc_s[...] = jnp.zeros(acc_s.shape, F32)

    k = k_ref[...].astype(BF16)
    v = v_ref[...].astype(BF16)
    kpos = ki * tk + lax.broadcasted_iota(jnp.int32, (1, tk), 1)
    qpos_col = q0 + lax.broadcasted_iota(jnp.int32, (tq, 1), 0)
    valid = kpos <= qpos_col
    bias = slope * (kpos - q0).astype(F32)
    for mp in range(2):
        rows = slice(mp * tq, (mp + 1) * tq)
        s = jnp.where(valid, _nt(q2_s[rows, :], k) + bias, NEG)
        m_old = m_s[rows, :]
        m_new = jnp.maximum(m_old, jnp.max(s, axis=-1, keepdims=True))
        alpha = jnp.exp(m_old - m_new)
        p = jnp.where(valid, jnp.exp(s - m_new), 0.0)
        l_s[rows, :] = alpha * l_s[rows, :] + jnp.sum(p, axis=-1, keepdims=True)
        acc_s[rows, :] = alpha * acc_s[rows, :] + _nn(p.astype(BF16), v)
        m_s[rows, :] = m_new

    @pl.when(last_tab[step] == 1)
    def _fin():
        o = acc_s[...] / jnp.maximum(l_s[...], 1e-30)
        lam = _lambda_value(lam_ref[...], lam_init)
        o_ref[...] = _subln(o[:tq] - lam * o[tq:], subln_ref[...], lam_init)


def _diff_prompt(qd, kvd, lam_p, subln, lam_init, tq, tk):
    B, S, _ = qd.shape
    qi, ki, last = _causal_pairs(S, tq, tk)
    grid_spec = pltpu.PrefetchScalarGridSpec(
        num_scalar_prefetch=3,
        grid=(B, DIFF_HEADS, int(qi.shape[0])),
        in_specs=[
            pl.BlockSpec((None, tq, DIFF_V_DIM), lambda b, h, s, qt, kt, lt: (b, qt[s], h)),
            pl.BlockSpec((None, tk, DIFF_V_DIM), lambda b, h, s, qt, kt, lt: (b, kt[s], h)),
            pl.BlockSpec((None, tk, DIFF_V_DIM), lambda b, h, s, qt, kt, lt: (b, kt[s], DIFF_HEADS + h)),
            pl.BlockSpec(lam_p.shape, lambda b, h, s, qt, kt, lt: (0, 0)),
            pl.BlockSpec(subln.shape, lambda b, h, s, qt, kt, lt: (0, 0)),
        ],
        out_specs=pl.BlockSpec((None, tq, DIFF_V_DIM), lambda b, h, s, qt, kt, lt: (b, qt[s], h)),
        scratch_shapes=[
            pltpu.VMEM((2 * tq, LANES), BF16),
            pltpu.VMEM((2 * tq, 1), F32),
            pltpu.VMEM((2 * tq, 1), F32),
            pltpu.VMEM((2 * tq, DIFF_V_DIM), F32),
        ],
    )
    return pl.pallas_call(
        functools.partial(_diff_prompt_kernel, tq=tq, tk=tk, lam_init=lam_init),
        grid_spec=grid_spec,
        out_shape=jax.ShapeDtypeStruct((B, S, DIFF_WIDTH), F32),
        compiler_params=_cparams(("parallel", "parallel", "arbitrary")),
    )(qi, ki, last, qd, kvd, kvd, lam_p, subln)


def _merge_kernel(x_ref, oa_ref, ob_ref, gm_ref, wa_ref, wb_ref, wo_ref, h_ref):
    d = x_ref.shape[-1]
    ya = jnp.dot(oa_ref[...].astype(BF16), wa_ref[...], preferred_element_type=F32)
    yb = jnp.dot(ob_ref[...].astype(BF16), wb_ref[...], preferred_element_type=F32)
    y = gm_ref[:, :d] * ya + gm_ref[:, d:] * yb
    h_ref[...] = x_ref[...] + jnp.dot(y.astype(BF16), wo_ref[...], preferred_element_type=F32)


def _merge(x2d, oa, ob, gm, wa, wb, wo, tm):
    T, D = x2d.shape
    row = lambda w: pl.BlockSpec((tm, w), lambda i: (i, 0))
    full = lambda a: pl.BlockSpec(a.shape, lambda i: (0, 0))
    return pl.pallas_call(
        _merge_kernel,
        grid=(T // tm,),
        in_specs=[row(D), row(NSA_WIDTH), row(DIFF_WIDTH), row(2 * D), full(wa), full(wb), full(wo)],
        out_specs=row(D),
        out_shape=jax.ShapeDtypeStruct((T, D), F32),
        compiler_params=_cparams(("parallel",)),
    )(x2d, oa, ob, gm, wa, wb, wo)


def _ffn_kernel(*refs, tiles_per_seq, per_row_state):
    if per_row_state:
        (h_ref, gn_ref, wa_ref, wg_ref, cwa_ref, cwg_ref, cba_ref, cbg_ref, wd_ref, pa_ref, pg_ref,
         y_ref, ua_o, ug_o, xn_s, acc_s) = refs
    else:
        (h_ref, gn_ref, wa_ref, wg_ref, cwa_ref, cwg_ref, cba_ref, cbg_ref, wd_ref,
         y_ref, ua_o, ug_o, xn_s, acc_s, carry_s) = refs
    i, j = pl.program_id(0), pl.program_id(1)
    tm = h_ref.shape[0]

    @pl.when(j == 0)
    def _init():
        x = h_ref[...]
        ms = jnp.mean(x * x, axis=-1, keepdims=True)
        xn_s[...] = (x * lax.rsqrt(ms + RMS_EPS) * gn_ref[...]).astype(BF16)
        acc_s[...] = jnp.zeros(acc_s.shape, F32)

    xn = xn_s[...]
    row = lax.broadcasted_iota(jnp.int32, (tm, 1), 0)

    def conv(w_ref, cw_ref, cb_ref, prev_ref, part, u_out):
        u = jnp.dot(xn, w_ref[...], preferred_element_type=F32)
        cw = cw_ref[...]
        if per_row_state:
            prev = prev_ref[...]
            p2, p1 = prev[:, 0, :], prev[:, 1, :]
            u_out[:, 0, :] = p1
            u_out[:, 1, :] = u
        else:
            @pl.when((i % tiles_per_seq) == 0)
            def _zero_state():
                carry_s[j, part] = jnp.zeros((CONV_W - 1, u.shape[1]), F32)

            carry = carry_s[j, part]
            p1 = jnp.where(row == 0, carry[1:2, :], pltpu.roll(u, 1, 0))
            p2 = jnp.where(row == 0, carry[0:1, :], jnp.where(row == 1, carry[1:2, :], pltpu.roll(u, 2, 0)))
            tail = u[tm - 2:, :]
            carry_s[j, part] = tail
            u_out[...] = tail
        return cb_ref[...] + cw[2:3, :] * u + cw[1:2, :] * p1 + cw[0:1, :] * p2

    a = conv(wa_ref, cwa_ref, cba_ref, None if not per_row_state else pa_ref, 0, ua_o)
    g = conv(wg_ref, cwg_ref, cbg_ref, None if not per_row_state else pg_ref, 1, ug_o)
    act = (g * jax.nn.sigmoid(g) * a).astype(BF16)
    acc_s[...] += jnp.dot(act, wd_ref[...], preferred_element_type=F32)

    @pl.when(j == pl.num_programs(1) - 1)
    def _fin():
        y_ref[...] = h_ref[...] + acc_s[...]


def _ffn(h2d, gn, w_up, conv_w, conv_b, w_down, tm, fc, seq_len=None, state=None):
    T, D = h2d.shape
    dff = w_down.shape[0]
    nf = dff // fc
    per_row = state is not None
    cb2 = conv_b[None, :]
    in_specs = [
        pl.BlockSpec((tm, D), lambda i, j: (i, 0)),
        pl.BlockSpec(gn.shape, lambda i, j: (0, 0)),
        pl.BlockSpec((D, fc), lambda i, j: (0, j)),
        pl.BlockSpec((D, fc), lambda i, j: (0, nf + j)),
        pl.BlockSpec((CONV_W, fc), lambda i, j: (0, j)),
        pl.BlockSpec((CONV_W, fc), lambda i, j: (0, nf + j)),
        pl.BlockSpec((1, fc), lambda i, j: (0, j)),
        pl.BlockSpec((1, fc), lambda i, j: (0, nf + j)),
        pl.BlockSpec((fc, D), lambda i, j: (j, 0)),
    ]
    args = [h2d, gn, w_up, w_up, conv_w, conv_w, cb2, cb2, w_down]
    scratch = [pltpu.VMEM((tm, D), BF16), pltpu.VMEM((tm, D), F32)]
    if per_row:
        in_specs += [pl.BlockSpec((tm, CONV_W - 1, fc), lambda i, j: (i, 0, j)),
                     pl.BlockSpec((tm, CONV_W - 1, fc), lambda i, j: (i, 0, nf + j))]
        args += [state, state]
        n_state, tiles_per_seq = T, 1
        st_spec = pl.BlockSpec((tm, CONV_W - 1, fc), lambda i, j: (i, 0, j))
    else:
        tiles_per_seq = seq_len // tm
        n_state = T // tm
        scratch.append(pltpu.VMEM((nf, 2, CONV_W - 1, fc), F32))
        st_spec = pl.BlockSpec((None, CONV_W - 1, fc), lambda i, j: (i, 0, j))
    y, ua, ug = pl.pallas_call(
        functools.partial(_ffn_kernel, tiles_per_seq=tiles_per_seq, per_row_state=per_row),
        grid=(T // tm, nf),
        in_specs=in_specs,
        out_specs=[pl.BlockSpec((tm, D), lambda i, j: (i, 0)), st_spec, st_spec],
        out_shape=[jax.ShapeDtypeStruct((T, D), F32),
                   jax.ShapeDtypeStruct((n_state, CONV_W - 1, dff), F32),
                   jax.ShapeDtypeStruct((n_state, CONV_W - 1, dff), F32)],
        scratch_shapes=scratch,
        compiler_params=_cparams(("arbitrary", "arbitrary")),
    )(*args)
    state_out = jnp.concatenate([ua, ug], axis=-1)
    if not per_row:
        state_out = state_out[tiles_per_seq - 1::tiles_per_seq]
    return y, state_out


def _rows1(q_row, scale):
    return _to_rows(q_row, scale)


def _slope_col(fn, n, rep):
    return jnp.concatenate([jnp.full((1, 1), fn(h // rep), F32) for h in range(n * rep)], axis=0)


def _nsa_local_sample_kernel(q_ref, gate_ref, cmp_ref, win_ref, kvw_ref, loc_o, mask_o, *, past, w_buf, n_pages):
    nc = (past + 1) // NSA_BLOCK
    ns = -(-(past + 1) // NSA_BLOCK)
    n_sel = min(NSA_TOPK, ns)
    nsp = -(-ns // LANES) * LANES
    bpp = PAGE_SIZE // NSA_BLOCK
    q2 = _rows1(q_ref[...], ATTN_SCALE)
    q2b = q2.astype(BF16)
    gate = gate_ref[...]
    slope = _slope_col(_nsa_slope, NSA_HEADS, 1)

    cmp = cmp_ref[...][:nc]
    kc, vc = cmp[:, :LANES], cmp[:, LANES:]
    bend = lax.broadcasted_iota(jnp.int32, (1, nc), 1) * NSA_BLOCK + (NSA_BLOCK - 1)
    dist_c = past - bend
    valid_c = dist_c >= 0
    p_c = _softmax_rows(_nt(q2, kc, precision=HIGHEST) - slope * dist_c.astype(F32), valid_c)
    o_c = _nn(p_c, vc, precision=HIGHEST)

    lane = lax.broadcasted_iota(jnp.int32, (1, nsp), 1)
    cur = past // NSA_BLOCK
    forced = (lane == 0) | (lane == cur) | (lane == cur - 1)
    ii = lax.broadcasted_iota(jnp.int32, (nsp, nsp), 0)
    jj = lax.broadcasted_iota(jnp.int32, (nsp, nsp), 1)
    pg = lax.broadcasted_iota(jnp.int32, (n_pages, nsp), 0)
    nn = lax.broadcasted_iota(jnp.int32, (n_pages, nsp), 1)
    tok_lo = lax.broadcasted_iota(jnp.int32, (n_pages, PAGE_SIZE), 1) < NSA_BLOCK
    for g in range(NSA_KV_HEADS):
        imp = jnp.sum(p_c[g * NSA_GROUP:(g + 1) * NSA_GROUP], axis=0, keepdims=True)
        imp = jnp.concatenate([imp, jnp.zeros((1, nsp - nc), F32)], axis=1)
        score = jnp.where(forced, FORCE_SCORE, jnp.where(lane > cur, -1.0, imp))
        score = jnp.where(lane < ns, score, -3e38)
        col = jnp.sum(jnp.where(ii == jj, score, 0.0), axis=1, keepdims=True)
        ahead = (col > score) | ((col == score) & (ii < jj))
        cnt = jnp.sum(ahead.astype(jnp.int32), axis=0, keepdims=True)
        sel = ((cnt < n_sel) & (lane < ns)).astype(F32)
        c0 = jnp.sum(jnp.where(nn == pg * bpp, sel, 0.0), axis=1, keepdims=True)
        c1 = jnp.sum(jnp.where(nn == pg * bpp + 1, sel, 0.0), axis=1, keepdims=True)
        mask_o[g] = jnp.where(tok_lo, c0, c1)

    win = win_ref[...]
    kw, vw = win[:, :LANES].astype(BF16), win[:, LANES:].astype(BF16)
    kpos = (past - w_buf) + lax.broadcasted_iota(jnp.int32, (1, w_buf), 1)
    dist_w = past - kpos
    valid_w = (dist_w >= 0) & (dist_w < NSA_WINDOW) & (kpos >= 0)
    s_w = jnp.where(valid_w, _nt(q2b, kw) - slope * dist_w.astype(F32), NEG)
    new = kvw_ref[...]
    kn, vn = new[:, :LANES], new[:, LANES:]
    s_n = jnp.sum(q2 * kn, axis=-1, keepdims=True)
    m = jnp.maximum(jnp.max(s_w, axis=-1, keepdims=True), s_n)
    p_w = jnp.where(valid_w, jnp.exp(s_w - m), 0.0)
    p_n = jnp.exp(s_n - m)
    o_w = (_nn(p_w.astype(BF16), vw) + p_n * vn) / (jnp.sum(p_w, axis=-1, keepdims=True) + p_n)

    loc_o[...] = _gate_rows(gate, 0, 1) * o_c + _gate_rows(gate, 2, 1) * o_w


def _nsa_local_sample(q, gate, kvcmp, win_state, kvw_new, layer, past):
    DB = q.shape[0]
    nc_all = kvcmp.shape[1]
    w_buf = win_state.shape[2]
    n_pages = past // PAGE_SIZE
    return pl.pallas_call(
        functools.partial(_nsa_local_sample_kernel, past=past, w_buf=w_buf, n_pages=n_pages),
        grid=(DB,),
        in_specs=[
            pl.BlockSpec((None, 1, NSA_WIDTH), lambda b: (b, 0, 0)),
            pl.BlockSpec((None, 1, GATE_PAD), lambda b: (b, 0, 0)),
            pl.BlockSpec((None, nc_all, KV_W), lambda b: (b, 0, 0)),
            pl.BlockSpec((None, None, w_buf, KV_W), lambda b: (layer, b, 0, 0)),
            pl.BlockSpec((None, 1, KV_W), lambda b: (b, 0, 0)),
        ],
        out_specs=[pl.BlockSpec((None, NSA_HEADS, LANES), lambda b: (b, 0, 0)),
                   pl.BlockSpec((None, NSA_KV_HEADS, n_pages, PAGE_SIZE), lambda b: (b, 0, 0, 0))],
        out_shape=[jax.ShapeDtypeStruct((DB, NSA_HEADS, LANES), F32),
                   jax.ShapeDtypeStruct((DB, NSA_KV_HEADS, n_pages, PAGE_SIZE), F32)],
        compiler_params=_cparams(("parallel",)),
    )(q, gate, kvcmp, win_state, kvw_new)


def _nsa_sel_sample_kernel(*refs, ppc, past):
    pt_ref = refs[0]
    pages = refs[1:1 + ppc]
    q_ref, gate_ref, new_ref, mask_ref, loc_ref, o_ref, q2_s, m_s, l_s, acc_s = refs[1 + ppc:]
    del pt_ref
    c = pl.program_id(1)
    slope = _slope_col(_nsa_slope, NSA_HEADS, 1)

    @pl.when(c == 0)
    def _init():
        q2_s[...] = _rows1(q_ref[...], ATTN_SCALE)
        m_s[...] = jnp.full(m_s.shape, NEG, F32)
        l_s[...] = jnp.zeros(l_s.shape, F32)
        acc_s[...] = jnp.zeros(acc_s.shape, F32)

    q2 = q2_s[...]
    q2b = q2.astype(BF16)
    tiles = [r[...] for r in pages]
    ks = jnp.concatenate([t[:, :LANES] for t in tiles], axis=0).astype(BF16)
    vs = jnp.concatenate([t[:, LANES:] for t in tiles], axis=0).astype(BF16)
    n = ppc * PAGE_SIZE
    kpos = c * n + lax.broadcasted_iota(jnp.int32, (1, n), 1)
    dist = (past - kpos).astype(F32)
    mask = mask_ref[...]
    vrow = [jnp.concatenate([mask[g, i:i + 1, :] for i in range(ppc)], axis=1) for g in range(NSA_KV_HEADS)]
    second_group = lax.broadcasted_iota(jnp.int32, (NSA_HEADS, n), 0) >= NSA_GROUP
    valid = jnp.where(second_group, vrow[1], vrow[0]) > 0.5
    s = jnp.where(valid, _nt(q2b, ks) - slope * dist, NEG)
    m_old = m_s[...]
    m_new = jnp.maximum(m_old, jnp.max(s, axis=-1, keepdims=True))
    alpha = jnp.exp(m_old - m_new)
    p = jnp.where(valid, jnp.exp(s - m_new), 0.0)
    l_s[...] = alpha * l_s[...] + jnp.sum(p, axis=-1, keepdims=True)
    acc_s[...] = alpha * acc_s[...] + _nn(p.astype(BF16), vs)
    m_s[...] = m_new

    @pl.when(c == pl.num_programs(1) - 1)
    def _fin():
        new = new_ref[...]
        kn, vn = new[:, :LANES], new[:, LANES:]
        s_n = jnp.sum(q2 * kn, axis=-1, keepdims=True)
        m_old = m_s[...]
        m_new = jnp.maximum(m_old, s_n)
        alpha = jnp.exp(m_old - m_new)
        p_n = jnp.exp(s_n - m_new)
        o_s = (alpha * acc_s[...] + p_n * vn) / (alpha * l_s[...] + p_n)
        o_ref[...] = _from_rows(loc_ref[...] + _gate_rows(gate_ref[...], 1, 1) * o_s, 1)


def _nsa_sel_sample(pool, page_table, q, gate, kvs_new, mask, loc, layer, past, ppc):
    DB, n_pages = page_table.shape
    grid_spec = pltpu.PrefetchScalarGridSpec(
        num_scalar_prefetch=1,
        grid=(DB, n_pages // ppc),
        in_specs=_page_specs(ppc, layer, KV_W) + [
            pl.BlockSpec((None, 1, NSA_WIDTH), lambda b, c, pt: (b, 0, 0)),
            pl.BlockSpec((None, 1, GATE_PAD), lambda b, c, pt: (b, 0, 0)),
            pl.BlockSpec((None, 1, KV_W), lambda b, c, pt: (b, 0, 0)),
            pl.BlockSpec((None, NSA_KV_HEADS, ppc, PAGE_SIZE), lambda b, c, pt: (b, 0, c, 0)),
            pl.BlockSpec((None, NSA_HEADS, LANES), lambda b, c, pt: (b, 0, 0)),
        ],
        out_specs=pl.BlockSpec((None, 1, NSA_WIDTH), lambda b, c, pt: (b, 0, 0)),
        scratch_shapes=[pltpu.VMEM((NSA_HEADS, LANES), F32), pltpu.VMEM((NSA_HEADS, 1), F32),
                        pltpu.VMEM((NSA_HEADS, 1), F32), pltpu.VMEM((NSA_HEADS, LANES), F32)],
    )
    return pl.pallas_call(
        functools.partial(_nsa_sel_sample_kernel, ppc=ppc, past=past),
        grid_spec=grid_spec,
        out_shape=jax.ShapeDtypeStruct((DB, 1, NSA_WIDTH), F32),
        compiler_params=_cparams(("parallel", "arbitrary")),
    )(page_table, *([pool] * ppc), q, gate, kvs_new, mask, loc)


def _diff_sample_kernel(*refs, ppc, past, lam_init):
    pages = refs[1:1 + ppc]
    q_ref, new_ref, lam_ref, subln_ref, o_ref, q2_s, m_s, l_s, acc_s = refs[1 + ppc:]
    c = pl.program_id(1)
    rows = 2 * DIFF_HEADS
    slope = _slope_col(_diff_slope, DIFF_HEADS, 2)

    @pl.when(c == 0)
    def _init():
        q = jnp.broadcast_to(q_ref[...] * ATTN_SCALE, (rows, DIFF_WIDTH))
        r = lax.broadcasted_iota(jnp.int32, (rows, DIFF_WIDTH), 0)
        ln = lax.broadcasted_iota(jnp.int32, (rows, DIFF_WIDTH), 1)
        q2_s[...] = jnp.where(ln // HEAD_DIM == r, q, 0.0)
        m_s[...] = jnp.full(m_s.shape, NEG, F32)
        l_s[...] = jnp.zeros(l_s.shape, F32)
        acc_s[...] = jnp.zeros(acc_s.shape, F32)

    q2 = q2_s[...]
    tiles = [r[...] for r in pages]
    k = jnp.concatenate([t[:, :DIFF_WIDTH] for t in tiles], axis=0).astype(BF16)
    v = jnp.concatenate([t[:, DIFF_WIDTH:] for t in tiles], axis=0).astype(BF16)
    n = ppc * PAGE_SIZE
    kpos = c * n + lax.broadcasted_iota(jnp.int32, (1, n), 1)
    s = _nt(q2.astype(BF16), k) - slope * (past - kpos).astype(F32)
    m_old = m_s[...]
    m_new = jnp.maximum(m_old, jnp.max(s, axis=-1, keepdims=True))
    alpha = jnp.exp(m_old - m_new)
    p = jnp.exp(s - m_new)
    l_s[...] = alpha * l_s[...] + jnp.sum(p, axis=-1, keepdims=True)
    acc_s[...] = alpha * acc_s[...] + _nn(p.astype(BF16), v)
    m_s[...] = m_new

    @pl.when(c == pl.num_programs(1) - 1)
    def _fin():
        new = new_ref[...]
        kn, vn = new[:, :DIFF_WIDTH], new[:, DIFF_WIDTH:]
        s_n = jnp.sum(q2 * kn, axis=-1, keepdims=True)
        m_old = m_s[...]
        m_new = jnp.maximum(m_old, s_n)
        alpha = jnp.exp(m_old - m_new)
        p_n = jnp.exp(s_n - m_new)
        o = (alpha * acc_s[...] + p_n * vn) / (alpha * l_s[...] + p_n)
        lam = _lambda_value(lam_ref[...], lam_init)
        outs = []
        for h in range(DIFF_HEADS):
            cols = slice(h * DIFF_V_DIM, (h + 1) * DIFF_V_DIM)
            a = o[2 * h:2 * h + 1, cols] - lam * o[2 * h + 1:2 * h + 2, cols]
            outs.append(_subln(a, subln_ref[...], lam_init))
        o_ref[...] = jnp.concatenate(outs, axis=1)


def _diff_sample(pool, page_table, qd, kvd_new, lam_p, subln, lam_init, layer, past, ppc):
    DB, n_pages = page_table.shape
    grid_spec = pltpu.PrefetchScalarGridSpec(
        num_scalar_prefetch=1,
        grid=(DB, n_pages // ppc),
        in_specs=_page_specs(ppc, layer, 2 * DIFF_WIDTH) + [
            pl.BlockSpec((None, 1, DIFF_WIDTH), lambda b, c, pt: (b, 0, 0)),
            pl.BlockSpec((None, 1, 2 * DIFF_WIDTH), lambda b, c, pt: (b, 0, 0)),
            pl.BlockSpec(lam_p.shape, lambda b, c, pt: (0, 0)),
            pl.BlockSpec(subln.shape, lambda b, c, pt: (0, 0)),
        ],
        out_specs=pl.BlockSpec((None, 1, DIFF_WIDTH), lambda b, c, pt: (b, 0, 0)),
        scratch_shapes=[pltpu.VMEM((2 * DIFF_HEADS, DIFF_WIDTH), F32), pltpu.VMEM((2 * DIFF_HEADS, 1), F32),
                        pltpu.VMEM((2 * DIFF_HEADS, 1), F32), pltpu.VMEM((2 * DIFF_HEADS, DIFF_WIDTH), F32)],
    )
    return pl.pallas_call(
        functools.partial(_diff_sample_kernel, ppc=ppc, past=past, lam_init=lam_init),
        grid_spec=grid_spec,
        out_shape=jax.ShapeDtypeStruct((DB, 1, DIFF_WIDTH), F32),
        compiler_params=_cparams(("parallel", "arbitrary")),
    )(page_table, *([pool] * ppc), qd, kvd_new, lam_p, subln)


def _pick(n, prefs):
    for p in prefs:
        if n % p == 0:
            return p
    return n


def _ffn_chunk(dff):
    for k in (2, 1, 4, 11, 22):
        if dff % k == 0 and (dff // k) % LANES == 0 and dff // k <= 1536:
            return dff // k
    return dff


def kernel(x_prompt, x_sample, cache_nsa_cmp, cache_nsa_slc, cache_diff, state_nsa_win, state_ffn_conv,
           page_table, norm_attn, w_in, qk_gain_nsa, qk_gain_diff, nsa_cmp_w, diff_lambda, diff_subln,
           w_br_a, w_br_b, w_o, norm_ffn, w_up, conv_w, conv_b, w_down):
    B, S, D = x_prompt.shape
    DB, t_new, _ = x_sample.shape
    assert t_new == 1, "the sample group carries one new token per sequence"
    depth = w_in.shape[0]
    n_pages = page_table.shape[1]
    past = n_pages * PAGE_SIZE
    n_phys = cache_nsa_cmp.shape[0]
    w_buf = state_nsa_win.shape[2]
    dff = w_down.shape[1]
    assert S % PAGE_SIZE == 0 and S >= NSA_WINDOW + PAGE_SIZE

    tm_proj = _pick(S, (256, 128))
    tm_row = _pick(S, (512, 256, 128))
    tq_nsa = 128
    tk_nsa = _pick(S, (512, 256, 128))
    tq_diff = _pick(S, (256, 128))
    tk_diff = _pick(S, (512, 256, 128))
    fc = _ffn_chunk(dff)
    ppc = _pick(n_pages, (8, 4, 2, 1))
    ppc_cmp = _pick(n_pages, (4, 2, 1))

    pool_cmp = cache_nsa_cmp.reshape(n_phys, depth, PAGE_SIZE, KV_W)
    pool_slc = cache_nsa_slc.reshape(n_phys, depth, PAGE_SIZE, KV_W)
    pool_diff = cache_diff.reshape(n_phys, depth, PAGE_SIZE, 2 * DIFF_WIDTH)
    win_state = state_nsa_win.reshape(depth, DB, w_buf, KV_W)
    pmat = _avg_matrix()

    xp = x_prompt.reshape(B * S, D)
    xs = x_sample.reshape(DB, D)
    outs_p = {k: [] for k in ("cmp", "slc", "diff", "win", "conv")}
    outs_s = {k: [] for k in ("cmp", "slc", "diff", "win", "conv")}
    w_keep_p = min(NSA_WINDOW, S)
    w_keep_s = min(NSA_WINDOW, w_buf + 1)

    for l in range(depth):
        lam_init = 0.8 - 0.6 * math.exp(-0.3 * l)
        w_packed = _pack_w_in(w_in[l])
        gains = _pack_gains(qk_gain_nsa[l], qk_gain_diff[l])
        gn_a = norm_attn[l][None, :]
        gn_f = norm_ffn[l][None, :]
        wexp = _cmp_weights(nsa_cmp_w[l])
        wa, wb, wo = w_br_a[l].astype(BF16), w_br_b[l].astype(BF16), w_o[l].astype(BF16)
        wup, wdn = w_up[l].astype(BF16), w_down[l].astype(BF16)
        lam_p = diff_lambda[l].astype(F32)
        subln = diff_subln[l][None, :]

        q, kvc, kvs, kvw, gate, qd, kvd, gm = _proj(xp, gn_a, w_packed, gains, pmat, tm_proj)
        r3 = lambda a: a.reshape(B, S, a.shape[-1])
        kvcmp = _compress_prompt(r3(kvc), wexp, tk_nsa)
        o_a = _nsa_prompt(r3(q), r3(gate), kvcmp, r3(kvw), r3(kvs), tq_nsa, tk_nsa)
        o_b = _diff_prompt(r3(qd), r3(kvd), lam_p, subln, lam_init, tq_diff, tk_diff)
        hp = _merge(xp, o_a.reshape(B * S, -1), o_b.reshape(B * S, -1), gm, wa, wb, wo, tm_row)
        xp, conv_p = _ffn(hp, gn_f, wup, conv_w[l], conv_b[l], wdn, tm_row, fc, seq_len=S)
        outs_p["cmp"].append(kvc.reshape(B, S, 2, NSA_KV_HEADS, HEAD_DIM))
        outs_p["slc"].append(kvs.reshape(B, S, 2, NSA_KV_HEADS, HEAD_DIM))
        outs_p["diff"].append(kvd.reshape(B, S, 2, DIFF_HEADS, DIFF_V_DIM))
        outs_p["win"].append(r3(kvw)[:, S - w_keep_p:].reshape(B, w_keep_p, 2, NSA_KV_HEADS, HEAD_DIM))
        outs_p["conv"].append(conv_p)

        q, kvc, kvs, kvw, gate, qd, kvd, gm = _proj(xs, gn_a, w_packed, gains, pmat, DB)
        r1 = lambda a: a.reshape(DB, 1, a.shape[-1])
        kvcmp = _compress_pages(pool_cmp, page_table, wexp, l, ppc_cmp)
        loc, mask = _nsa_local_sample(r1(q), r1(gate), kvcmp, win_state, r1(kvw), l, past)
        o_a = _nsa_sel_sample(pool_slc, page_table, r1(q), r1(gate), r1(kvs), mask, loc, l, past, ppc)
        o_b = _diff_sample(pool_diff, page_table, r1(qd), r1(kvd), lam_p, subln, lam_init, l, past, ppc)
        hs = _merge(xs, o_a.reshape(DB, -1), o_b.reshape(DB, -1), gm, wa, wb, wo, DB)
        xs, conv_s = _ffn(hs, gn_f, wup, conv_w[l], conv_b[l], wdn, DB, fc, state=state_ffn_conv[l])
        outs_s["cmp"].append(kvc.reshape(DB, 1, 2, NSA_KV_HEADS, HEAD_DIM))
        outs_s["slc"].append(kvs.reshape(DB, 1, 2, NSA_KV_HEADS, HEAD_DIM))
        outs_s["diff"].append(kvd.reshape(DB, 1, 2, DIFF_HEADS, DIFF_V_DIM))
        win_all = jnp.concatenate([win_state[l], r1(kvw)], axis=1)
        outs_s["win"].append(win_all[:, w_buf + 1 - w_keep_s:].reshape(DB, w_keep_s, 2, NSA_KV_HEADS, HEAD_DIM))
        outs_s["conv"].append(conv_s)

    return (xp.reshape(B, S, D), xs.reshape(DB, 1, D),
            jnp.stack(outs_p["cmp"], axis=1), jnp.stack(outs_p["slc"], axis=1), jnp.stack(outs_p["diff"], axis=1),
            jnp.stack(outs_p["win"], axis=0), jnp.stack(outs_p["conv"], axis=0),
            jnp.stack(outs_s["cmp"], axis=1), jnp.stack(outs_s["slc"], axis=1), jnp.stack(outs_s["diff"], axis=1),
            jnp.stack(outs_s["win"], axis=0), jnp.stack(outs_s["conv"], axis=0))
```

```python
import functools
import math

import numpy as np
import jax
import jax.numpy as jnp
from jax import lax
from jax.experimental import pallas as pl
from jax.experimental.pallas import tpu as pltpu

HEAD_DIM = 64
NSA_HEADS = 8
NSA_KV_HEADS = 2
NSA_GROUP = NSA_HEADS // NSA_KV_HEADS
NSA_BLOCK = 64
NSA_TOPK = 16
NSA_WINDOW = 512
NSA_WIDTH = NSA_HEADS * HEAD_DIM
DIFF_HEADS = 4
DIFF_V_DIM = 2 * HEAD_DIM
DIFF_WIDTH = DIFF_HEADS * DIFF_V_DIM
CONV_W = 3
PAGE_SIZE = 128
RMS_EPS = 1e-6
ATTN_SCALE = HEAD_DIM ** -0.5
FORCE_SCORE = 1e4
KV_W = 2 * NSA_KV_HEADS * HEAD_DIM
LANES = 128
GATE_PAD = LANES
NEG = -1e30
VMEM_LIMIT = 56 * 1024 * 1024

F32 = jnp.float32
BF16 = jnp.bfloat16
HIGHEST = lax.Precision.HIGHEST

C_Q = 0
C_KVC = C_Q + NSA_WIDTH
C_KVS = C_KVC + KV_W
C_KVW = C_KVS + KV_W
C_GATE = C_KVW + KV_W
C_QD = C_GATE + GATE_PAD
C_KD = C_QD + DIFF_WIDTH
C_VD = C_KD + DIFF_WIDTH
C_GM = C_VD + DIFF_WIDTH


def _nsa_slope(h):
    g, r = h // NSA_GROUP, h % NSA_GROUP
    return 2.0 ** (-8.0 * (r * NSA_KV_HEADS + g + 1) / NSA_HEADS)


def _diff_slope(h):
    return 2.0 ** (-8.0 * (h + 1) / DIFF_HEADS)


LOG2E = math.log2(math.e)
MASK_BIG = 2.0 ** 100
AUX_BIAS = HEAD_DIM


def _split3(x):
    parts, rest = [], np.float64(x)
    for _ in range(3):
        p = np.float64(np.asarray(rest, dtype=np.float32).astype(jnp.bfloat16).astype(np.float32))
        parts.append(float(p))
        rest = rest - p
    return parts


def _bias_q_rows(slopes):
    rows = np.zeros((len(slopes), LANES), np.float32)
    for i, s in enumerate(slopes):
        rows[i, AUX_BIAS:AUX_BIAS + 6] = _split3(s * LOG2E) * 2
    return jnp.asarray(rows)


def _key_aux(n_keys):
    kpos = np.arange(n_keys)
    aux = np.zeros((n_keys, LANES), np.float32)
    blk = kpos // NSA_BLOCK
    ok = blk < AUX_BIAS
    aux[kpos[ok], blk[ok]] = 1.0
    aux[:, AUX_BIAS:AUX_BIAS + 3] = (NSA_BLOCK * blk)[:, None]
    aux[:, AUX_BIAS + 3:AUX_BIAS + 6] = (kpos % NSA_BLOCK)[:, None]
    return jnp.asarray(aux, dtype=BF16)


FLASH_ROWS = 1024


def _flash_update(qc_s, kc, v, m_s, l_s, acc_s, causal=None, groups=1):
    rows = qc_s.shape[0]
    qc, m_all, l_all, acc_all = qc_s[...], m_s[...], l_s[...], acc_s[...]
    m_out, l_out, acc_out = [], [], []
    chunk = min(FLASH_ROWS, rows)
    for r0 in range(0, rows, chunk):
        rs = slice(r0, r0 + chunk)
        s = _nt(qc[rs, :], kc)
        if causal is not None:
            t, n = causal.shape
            if chunk >= t:
                s = jnp.where(causal[None], s.reshape(chunk // t, t, n), NEG).reshape(chunk, n)
            else:
                s = jnp.where(causal[r0 % t:r0 % t + chunk], s, NEG)
        m_old = m_all[rs, :]
        m_new = jnp.maximum(m_old, jnp.max(s, axis=-1, keepdims=True))
        alpha = jnp.exp2(m_old - m_new)
        p = jnp.exp2(s - jnp.concatenate([m_new] * (s.shape[1] // LANES), axis=1))
        l_out.append(alpha * l_all[rs, :] + jnp.sum(p, axis=-1, keepdims=True))
        acc_out.append(alpha * acc_all[rs, :] + _nn(p.astype(BF16), v))
        m_out.append(m_new)
    m_s[...] = jnp.concatenate(m_out, axis=0)
    l_s[...] = jnp.concatenate(l_out, axis=0)
    acc_s[...] = jnp.concatenate(acc_out, axis=0)


def _nt(a, b, **kw):
    return lax.dot_general(a, b, (((1,), (1,)), ((), ())), preferred_element_type=F32, **kw)


def _nn(a, b, **kw):
    return lax.dot_general(a, b, (((1,), (0,)), ((), ())), preferred_element_type=F32, **kw)


def _cparams(sem):
    return pltpu.CompilerParams(dimension_semantics=sem, vmem_limit_bytes=VMEM_LIMIT)


def _proj_kernel(x_ref, gn_ref, w_ref, gains_ref, pmat_ref,
                 q_o, kvc_o, kvs_o, kvw_o, gate_o, qd_o, kvd_o, gm_o):
    x = x_ref[...]
    ms = jnp.mean(x * x, axis=-1, keepdims=True)
    xn = (x * lax.rsqrt(ms + RMS_EPS) * gn_ref[...]).astype(BF16)
    pmat = pmat_ref[...]

    def seg(c0, width):
        return jnp.dot(xn, w_ref[:, c0:c0 + width], preferred_element_type=F32)

    def headnorm(z, gain_row):
        zz = z * z
        hi = zz.astype(BF16)
        lo = (zz - hi.astype(F32)).astype(BF16)
        msq = jnp.dot(hi, pmat, preferred_element_type=F32) + jnp.dot(lo, pmat, preferred_element_type=F32)
        return z * lax.rsqrt(msq + RMS_EPS) * gains_ref[gain_row:gain_row + 1, :]

    def normed(c0, width, gain_row):
        z = seg(c0, width)
        return jnp.concatenate([headnorm(z[:, c:c + LANES], gain_row) for c in range(0, width, LANES)], axis=1)

    q_o[...] = normed(C_Q, NSA_WIDTH, 0)
    for c0, out, row in ((C_KVC, kvc_o, 1), (C_KVS, kvs_o, 2), (C_KVW, kvw_o, 3)):
        z = seg(c0, KV_W)
        out[...] = jnp.concatenate([headnorm(z[:, :LANES], row), z[:, LANES:]], axis=1)
    gate_o[...] = jax.nn.sigmoid(seg(C_GATE, GATE_PAD))
    qd_o[...] = normed(C_QD, DIFF_WIDTH, 4)
    kvd_o[:, :DIFF_WIDTH] = normed(C_KD, DIFF_WIDTH, 5)
    kvd_o[:, DIFF_WIDTH:] = seg(C_VD, DIFF_WIDTH)
    d_model = x.shape[-1]
    for c in range(0, 2 * d_model, 512):
        gm_o[:, c:c + 512] = jax.nn.sigmoid(seg(C_GM + c, 512))


def _proj(x2d, gn, w_packed, gains, pmat, tm):
    T, D = x2d.shape
    widths = (NSA_WIDTH, KV_W, KV_W, KV_W, GATE_PAD, DIFF_WIDTH, 2 * DIFF_WIDTH, 2 * D)
    full = lambda a: pl.BlockSpec(a.shape, lambda i: (0, 0))
    return pl.pallas_call(
        _proj_kernel,
        grid=(T // tm,),
        in_specs=[pl.BlockSpec((tm, D), lambda i: (i, 0)), full(gn), full(w_packed), full(gains), full(pmat)],
        out_specs=[pl.BlockSpec((tm, w), lambda i: (i, 0)) for w in widths],
        out_shape=[jax.ShapeDtypeStruct((T, w), F32) for w in widths],
        compiler_params=_cparams(("parallel",)),
    )(x2d, gn, w_packed, gains, pmat)


def _pack_w_in(w_in_l):
    D = w_in_l.shape[0]
    sizes = (NSA_WIDTH, KV_W, KV_W, KV_W, 3 * NSA_HEADS, DIFF_WIDTH, DIFF_WIDTH, DIFF_WIDTH, 2 * D)
    offs = np.cumsum(sizes)[:-1].tolist()
    q, kvc, kvs, kvw, gn, qd, kd, vd, gm = jnp.split(w_in_l, offs, axis=1)
    gn = jnp.pad(gn, ((0, 0), (0, GATE_PAD - gn.shape[1])))
    perm = lambda w: w.reshape(D, 2, DIFF_HEADS, HEAD_DIM).transpose(0, 2, 1, 3).reshape(D, DIFF_WIDTH)
    return jnp.concatenate([q, kvc, kvs, kvw, gn, perm(qd), perm(kd), vd, gm], axis=1).astype(BF16)


def _pack_gains(g_nsa, g_diff):
    two = lambda a, b: jnp.concatenate([a, b])[None, :]
    rows = [two(g_nsa[0], g_nsa[0]), two(g_nsa[1], g_nsa[1]), two(g_nsa[2], g_nsa[2]), two(g_nsa[3], g_nsa[3]),
            two(g_diff[0], g_diff[1]), two(g_diff[2], g_diff[3])]
    rows += [jnp.ones((1, LANES), F32)] * 2
    return jnp.concatenate(rows, axis=0).astype(F32)


def _avg_matrix():
    i = np.arange(LANES)
    return jnp.asarray((i[:, None] // HEAD_DIM == i[None, :] // HEAD_DIM) / HEAD_DIM, dtype=BF16)


def _compress_kernel(*refs, n_in, n_prefetch):
    refs = refs[n_prefetch:]
    w_ref, out_ref = refs[n_in], refs[n_in + 1]
    w = w_ref[...]
    outs = []
    for r in refs[:n_in]:
        x = r[...].reshape(-1, NSA_BLOCK, KV_W)
        outs.append(jnp.sum(x * w[None], axis=1))
    out_ref[...] = jnp.concatenate(outs, axis=0).reshape(out_ref.shape)


def _cmp_weights(cmp_w_l):
    return jnp.concatenate([jnp.broadcast_to(cmp_w_l[0][:, None], (NSA_BLOCK, KV_W // 2)),
                            jnp.broadcast_to(cmp_w_l[1][:, None], (NSA_BLOCK, KV_W // 2))], axis=1).astype(F32)


def _compress_prompt(kvc, wexp, tk):
    B, S, _ = kvc.shape
    nb = tk // NSA_BLOCK
    return pl.pallas_call(
        functools.partial(_compress_kernel, n_in=1, n_prefetch=0),
        grid=(B, S // tk),
        in_specs=[pl.BlockSpec((None, tk, KV_W), lambda b, i: (b, i, 0)),
                  pl.BlockSpec(wexp.shape, lambda b, i: (0, 0))],
        out_specs=pl.BlockSpec((None, nb, KV_W), lambda b, i: (b, i, 0)),
        out_shape=jax.ShapeDtypeStruct((B, S // NSA_BLOCK, KV_W), F32),
        compiler_params=_cparams(("parallel", "parallel")),
    )(kvc, wexp)


def _page_specs(n, layer, rows, cols):
    def mk(i):
        return pl.BlockSpec((None, None, rows, cols), lambda b, c, pt: (pt[b, c * n + i], layer, 0, 0))
    return [mk(i) for i in range(n)]


def _split_bf16(x):
    hi = x.astype(BF16)
    return hi, (x - hi.astype(F32)).astype(BF16)


def _compress_pages_kernel(*refs, ppc):
    pages, w_ref, out_ref = refs[1:1 + ppc], refs[1 + ppc], refs[2 + ppc]
    w_hi, w_lo = _split_bf16(w_ref[...])
    k_cols = lax.broadcasted_iota(jnp.int32, (2, KV_W), 1) < KV_W // 2
    outs = []
    for r in pages:
        hi, lo = _split_bf16(r[...])
        o8 = _nt(w_hi, hi) + _nt(w_lo, hi) + _nt(w_hi, lo)
        outs.append(jnp.where(k_cols, o8[0:2], o8[2:4]))
    out_ref[...] = jnp.concatenate(outs, axis=0)


def _cmp_page_weights(cmp_w_l):
    z = jnp.zeros((NSA_BLOCK,), F32)
    rows = [jnp.concatenate([cmp_w_l[0], z]), jnp.concatenate([z, cmp_w_l[0]]),
            jnp.concatenate([cmp_w_l[1], z]), jnp.concatenate([z, cmp_w_l[1]])]
    return jnp.concatenate([jnp.stack(rows), jnp.zeros((4, PAGE_SIZE), F32)], axis=0).astype(F32)


def _compress_pages(pool_t, page_table, w8, layer, ppc):
    DB, n_pages = page_table.shape
    bpp = PAGE_SIZE // NSA_BLOCK
    grid_spec = pltpu.PrefetchScalarGridSpec(
        num_scalar_prefetch=1,
        grid=(DB, n_pages // ppc),
        in_specs=_page_specs(ppc, layer, KV_W, PAGE_SIZE) + [pl.BlockSpec(w8.shape, lambda b, c, pt: (0, 0))],
        out_specs=pl.BlockSpec((None, ppc * bpp, KV_W), lambda b, c, pt: (b, c, 0)),
    )
    return pl.pallas_call(
        functools.partial(_compress_pages_kernel, ppc=ppc),
        grid_spec=grid_spec,
        out_shape=jax.ShapeDtypeStruct((DB, n_pages * bpp, KV_W), F32),
        compiler_params=_cparams(("parallel", "parallel")),
    )(page_table, *([pool_t] * ppc), w8)


def _to_rows(q, scale):
    t = q.shape[0]
    hi_half = lax.broadcasted_iota(jnp.int32, (t, LANES), 1) >= HEAD_DIM
    pieces = []
    for h in range(NSA_HEADS):
        g = h // NSA_GROUP
        c = q[:, (h // 2) * LANES:(h // 2 + 1) * LANES]
        if (h % 2) != g:
            c = pltpu.roll(c, HEAD_DIM, 1)
        keep = hi_half if g == 1 else jnp.logical_not(hi_half)
        pieces.append(jnp.where(keep, c * scale, 0.0))
    return jnp.concatenate(pieces, axis=0)


def _from_rows(o, t):
    lo_half = lax.broadcasted_iota(jnp.int32, (t, LANES), 1) < HEAD_DIM
    chunks = []
    for j in range(NSA_HEADS // 2):
        a = o[(2 * j) * t:(2 * j + 1) * t]
        b = o[(2 * j + 1) * t:(2 * j + 2) * t]
        if j // 2 == 0:
            chunks.append(jnp.where(lo_half, a, pltpu.roll(b, HEAD_DIM, 1)))
        else:
            chunks.append(jnp.where(lo_half, pltpu.roll(a, HEAD_DIM, 1), b))
    return jnp.concatenate(chunks, axis=1)


def _gate_rows(gate, branch, t):
    cols = [jnp.broadcast_to(gate[:, branch * NSA_HEADS + h:branch * NSA_HEADS + h + 1], (t, LANES))
            for h in range(NSA_HEADS)]
    return jnp.concatenate(cols, axis=0)


def _softmax_rows(s, valid):
    m = jnp.max(jnp.where(valid, s, NEG), axis=-1, keepdims=True)
    p = jnp.where(valid, jnp.exp(s - m), 0.0)
    return p / jnp.maximum(jnp.sum(p, axis=-1, keepdims=True), 1e-30)


def _nsa_prompt_kernel(qi_tab, ki_tab, last_tab, q_ref, gate_ref, cmp_ref, kvw_ref, kaux_all_ref, bq_ref,
                       kvs_ref, kaux_ref, o_ref, qc_s, loc_s, g1_s, m_s, l_s, acc_s, *, tq, tk, seq):
    step = pl.program_id(1)
    qi, ki = qi_tab[step], ki_tab[step]
    q0 = qi * tq
    nc = seq // NSA_BLOCK
    ns = -(-seq // NSA_BLOCK)
    n_sel = min(NSA_TOPK, ns)
    wk = NSA_WINDOW + tq

    @pl.when(ki == 0)
    def _init():
        q2 = _to_rows(q_ref[...], ATTN_SCALE)
        gate = gate_ref[...]
        qpos_col = q0 + lax.broadcasted_iota(jnp.int32, (tq, 1), 0)
        qpos_row = q0 + lax.broadcasted_iota(jnp.int32, (1, tq), 1)

        cmp = cmp_ref[...]
        kc, vc = cmp[:, :LANES], cmp[:, LANES:]
        bend_row = lax.broadcasted_iota(jnp.int32, (1, nc), 1) * NSA_BLOCK + (NSA_BLOCK - 1)
        dist_c = qpos_col - bend_row
        valid_c = dist_c >= 0
        s_c = _nt(q2.astype(BF16), kc.astype(BF16))
        vcb = vc.astype(BF16)
        o_c = []
        for h in range(NSA_HEADS):
            sh = s_c[h * tq:(h + 1) * tq] - _nsa_slope(h) * dist_c.astype(F32)
            o_c.append(_nn(_softmax_rows(sh, valid_c).astype(BF16), vcb))
        o_c = jnp.concatenate(o_c, axis=0)

        bend_col = lax.broadcasted_iota(jnp.int32, (nc, 1), 0) * NSA_BLOCK + (NSA_BLOCK - 1)
        dist_t = qpos_row - bend_col
        valid_t = dist_t >= 0
        s_t = _nt(kc, q2, precision=HIGHEST)
        blk = lax.broadcasted_iota(jnp.int32, (ns, tq), 0)
        cur = qpos_row // NSA_BLOCK
        sel_t = []
        for g in range(NSA_KV_HEADS):
            imp = jnp.zeros((nc, tq), F32)
            for r in range(NSA_GROUP):
                h = g * NSA_GROUP + r
                sh = s_t[:, h * tq:(h + 1) * tq] - _nsa_slope(h) * dist_t.astype(F32)
                m = jnp.max(jnp.where(valid_t, sh, NEG), axis=0, keepdims=True)
                p = jnp.where(valid_t, jnp.exp(sh - m), 0.0)
                imp = imp + p / jnp.maximum(jnp.sum(p, axis=0, keepdims=True), 1e-30)
            if ns > nc:
                imp = jnp.concatenate([imp, jnp.zeros((ns - nc, tq), F32)], axis=0)
            forced = (blk == 0) | (blk == cur) | (blk == cur - 1)
            score = jnp.where(forced, FORCE_SCORE, jnp.where(blk > cur, -1.0, imp))
            grp = [score[j:j + 8] for j in range(0, ns, 8)]
            below = lax.broadcasted_iota(jnp.int32, (8, tq), 0)
            cnt = [jnp.zeros((8, tq), jnp.int32) for _ in grp]
            for i in range(ns):
                row = jnp.broadcast_to(score[i:i + 1, :], (8, tq))
                for j, sc in enumerate(grp):
                    if 8 * j > i:
                        ahead = row >= sc
                    elif 8 * j + 7 <= i:
                        ahead = row > sc
                    else:
                        ahead = (row > sc) | ((row == sc) & (below > i - 8 * j))
                    cnt[j] = cnt[j] + ahead.astype(jnp.int32)
            sel_t.append((jnp.concatenate(cnt, axis=0) < n_sel).astype(F32))
        sel = jnp.concatenate(sel_t, axis=0).T

        q2l = (q2 * LOG2E).astype(BF16)
        low = lax.broadcasted_iota(jnp.int32, (tq, LANES), 1) < AUX_BIAS
        pad = jnp.ones((tq, LANES - ns), F32)
        aux_w, aux_s = [], []
        for h in range(NSA_HEADS):
            g = h // NSA_GROUP
            bq = jnp.broadcast_to(bq_ref[h:h + 1, :], (tq, LANES))
            drop = (jnp.concatenate([sel[:, g * ns:(g + 1) * ns], pad], axis=1) - 1.0) * MASK_BIG
            aux_w.append(jnp.where(low, 0.0, bq))
            aux_s.append(jnp.where(low, drop, bq))
        qc_w = jnp.concatenate([q2l, jnp.concatenate(aux_w, axis=0).astype(BF16)], axis=1)
        qc_s[...] = jnp.concatenate([q2l, jnp.concatenate(aux_s, axis=0).astype(BF16)], axis=1)

        start = pl.multiple_of(jnp.clip(q0 - NSA_WINDOW, 0, seq - wk), LANES)
        kvw = kvw_ref[pl.ds(start, wk), :]
        kcw = jnp.concatenate([kvw[:, :LANES].astype(BF16), kaux_all_ref[pl.ds(start, wk), :]], axis=1)
        vw = kvw[:, LANES:].astype(BF16)
        kpos = start + lax.broadcasted_iota(jnp.int32, (1, wk), 1)
        dist_w = qpos_col - kpos
        valid_w = ((dist_w >= 0) & (dist_w < NSA_WINDOW))[None]
        s_w = jnp.where(valid_w, _nt(qc_w, kcw).reshape(NSA_HEADS, tq, wk), NEG).reshape(NSA_HEADS * tq, wk)
        p_w = jnp.exp2(s_w - jnp.max(s_w, axis=-1, keepdims=True))
        o_w = _nn(p_w.astype(BF16), vw) / jnp.sum(p_w, axis=-1, keepdims=True)

        loc_s[...] = _gate_rows(gate, 0, tq) * o_c + _gate_rows(gate, 2, tq) * o_w
        g1_s[...] = _gate_rows(gate, 1, tq)
        m_s[...] = jnp.full(m_s.shape, NEG, F32)
        l_s[...] = jnp.zeros(l_s.shape, F32)
        acc_s[...] = jnp.zeros(acc_s.shape, F32)

    kvs = kvs_ref[...]
    kc_t = jnp.concatenate([kvs[:, :LANES].astype(BF16), kaux_ref[...]], axis=1)
    vs = kvs[:, LANES:].astype(BF16)
    is_last = last_tab[step] == 1

    @pl.when(jnp.logical_not(is_last))
    def _interior():
        _flash_update(qc_s, kc_t, vs, m_s, l_s, acc_s)

    @pl.when(is_last)
    def _diag():
        kpos = ki * tk + lax.broadcasted_iota(jnp.int32, (1, tk), 1)
        qpos = q0 + lax.broadcasted_iota(jnp.int32, (tq, 1), 0)
        _flash_update(qc_s, kc_t, vs, m_s, l_s, acc_s, causal=kpos <= qpos)
        o_ref[...] = _from_rows(loc_s[...] + g1_s[...] * (acc_s[...] / l_s[...]), tq)


def _causal_pairs(seq, tq, tk):
    assert tk % tq == 0, "only the last key tile of a query tile may cross the diagonal"
    qi, ki, last = [], [], []
    for i in range(seq // tq):
        n = (i * tq + tq - 1) // tk + 1
        for j in range(n):
            qi.append(i), ki.append(j), last.append(1 if j == n - 1 else 0)
    mk = lambda a: jnp.asarray(np.asarray(a, dtype=np.int32))
    return mk(qi), mk(ki), mk(last)


def _nsa_prompt(q, gate, kvcmp, kvw, kvs, kaux, tq, tk):
    B, S, _ = q.shape
    nc = S // NSA_BLOCK
    ns = -(-S // NSA_BLOCK)
    assert ns <= AUX_BIAS, "the block one-hot must fit below the ALiBi lanes"
    qi, ki, last = _causal_pairs(S, tq, tk)
    bq = _bias_q_rows([_nsa_slope(h) for h in range(NSA_HEADS)])
    rows = NSA_HEADS * tq
    grid_spec = pltpu.PrefetchScalarGridSpec(
        num_scalar_prefetch=3,
        grid=(B, int(qi.shape[0])),
        in_specs=[
            pl.BlockSpec((None, tq, NSA_WIDTH), lambda b, s, qt, kt, lt: (b, qt[s], 0)),
            pl.BlockSpec((None, tq, GATE_PAD), lambda b, s, qt, kt, lt: (b, qt[s], 0)),
            pl.BlockSpec((None, nc, KV_W), lambda b, s, qt, kt, lt: (b, 0, 0)),
            pl.BlockSpec((None, S, KV_W), lambda b, s, qt, kt, lt: (b, 0, 0)),
            pl.BlockSpec((S, LANES), lambda b, s, qt, kt, lt: (0, 0)),
            pl.BlockSpec(bq.shape, lambda b, s, qt, kt, lt: (0, 0)),
            pl.BlockSpec((None, tk, KV_W), lambda b, s, qt, kt, lt: (b, kt[s], 0)),
            pl.BlockSpec((tk, LANES), lambda b, s, qt, kt, lt: (kt[s], 0)),
        ],
        out_specs=pl.BlockSpec((None, tq, NSA_WIDTH), lambda b, s, qt, kt, lt: (b, qt[s], 0)),
        scratch_shapes=[
            pltpu.VMEM((rows, 2 * LANES), BF16),
            pltpu.VMEM((rows, LANES), F32),
            pltpu.VMEM((rows, LANES), F32),
            pltpu.VMEM((rows, LANES), F32),
            pltpu.VMEM((rows, LANES), F32),
            pltpu.VMEM((rows, LANES), F32),
        ],
    )
    return pl.pallas_call(
        functools.partial(_nsa_prompt_kernel, tq=tq, tk=tk, seq=S),
        grid_spec=grid_spec,
        out_shape=jax.ShapeDtypeStruct((B, S, NSA_WIDTH), F32),
        compiler_params=_cparams(("parallel", "arbitrary")),
    )(qi, ki, last, q, gate, kvcmp, kvw, kaux, bq, kvs, kaux)


def _lambda_value(lp, lam_init):
    a = jnp.sum(lp[0:1] * lp[1:2], axis=-1, keepdims=True)
    b = jnp.sum(lp[2:3] * lp[3:4], axis=-1, keepdims=True)
    return jnp.exp(a) - jnp.exp(b) + lam_init


def _subln(o, g, lam_init):
    ms = jnp.mean(o * o, axis=-1, keepdims=True)
    return o * lax.rsqrt(ms + RMS_EPS) * g * (1.0 - lam_init)


def _diff_prompt_kernel(qi_tab, ki_tab, last_tab, q_ref, k_ref, v_ref, kaux_ref, bq_ref, lam_ref, subln_ref, o_ref,
                        qc_s, m_s, l_s, acc_s, *, tq, tk, lam_init):
    step = pl.program_id(2)
    qi, ki = qi_tab[step], ki_tab[step]

    @pl.when(ki == 0)
    def _init():
        q = q_ref[...] * (ATTN_SCALE * LOG2E)
        lo_half = lax.broadcasted_iota(jnp.int32, (tq, LANES), 1) < HEAD_DIM
        q2 = jnp.concatenate([jnp.where(lo_half, q, 0.0), jnp.where(lo_half, 0.0, q)], axis=0)
        aux = jnp.broadcast_to(bq_ref[0:1, :], (2 * tq, LANES))
        qc_s[...] = jnp.concatenate([q2, aux], axis=1).astype(BF16)
        m_s[...] = jnp.full(m_s.shape, NEG, F32)
        l_s[...] = jnp.zeros(l_s.shape, F32)
        acc_s[...] = jnp.zeros(acc_s.shape, F32)

    kc = jnp.concatenate([k_ref[...].astype(BF16), kaux_ref[...]], axis=1)
    v = v_ref[...].astype(BF16)
    is_last = last_tab[step] == 1

    @pl.when(jnp.logical_not(is_last))
    def _interior():
        _flash_update(qc_s, kc, v, m_s, l_s, acc_s)

    @pl.when(is_last)
    def _diag():
        kpos = ki * tk + lax.broadcasted_iota(jnp.int32, (1, tk), 1)
        qpos = qi * tq + lax.broadcasted_iota(jnp.int32, (tq, 1), 0)
        _flash_update(qc_s, kc, v, m_s, l_s, acc_s, causal=kpos <= qpos)
        o = acc_s[...] / l_s[...]
        lam = _lambda_value(lam_ref[...], lam_init)
        o_ref[...] = _subln(o[:tq] - lam * o[tq:], subln_ref[...], lam_init)


def _diff_prompt(qd, kvd, kaux, lam_p, subln, lam_init, tq, tk):
    B, S, _ = qd.shape
    qi, ki, last = _causal_pairs(S, tq, tk)
    bq = jnp.repeat(_bias_q_rows([_diff_slope(h) for h in range(DIFF_HEADS)])[:, None, :], 8, axis=1)
    grid_spec = pltpu.PrefetchScalarGridSpec(
        num_scalar_prefetch=3,
        grid=(B, DIFF_HEADS, int(qi.shape[0])),
        in_specs=[
            pl.BlockSpec((None, tq, DIFF_V_DIM), lambda b, h, s, qt, kt, lt: (b, qt[s], h)),
            pl.BlockSpec((None, tk, DIFF_V_DIM), lambda b, h, s, qt, kt, lt: (b, kt[s], h)),
            pl.BlockSpec((None, tk, DIFF_V_DIM), lambda b, h, s, qt, kt, lt: (b, kt[s], DIFF_HEADS + h)),
            pl.BlockSpec((tk, LANES), lambda b, h, s, qt, kt, lt: (kt[s], 0)),
            pl.BlockSpec((None, 8, LANES), lambda b, h, s, qt, kt, lt: (h, 0, 0)),
            pl.BlockSpec(lam_p.shape, lambda b, h, s, qt, kt, lt: (0, 0)),
            pl.BlockSpec(subln.shape, lambda b, h, s, qt, kt, lt: (0, 0)),
        ],
        out_specs=pl.BlockSpec((None, tq, DIFF_V_DIM), lambda b, h, s, qt, kt, lt: (b, qt[s], h)),
        scratch_shapes=[
            pltpu.VMEM((2 * tq, 2 * LANES), BF16),
            pltpu.VMEM((2 * tq, LANES), F32),
            pltpu.VMEM((2 * tq, LANES), F32),
            pltpu.VMEM((2 * tq, DIFF_V_DIM), F32),
        ],
    )
    return pl.pallas_call(
        functools.partial(_diff_prompt_kernel, tq=tq, tk=tk, lam_init=lam_init),
        grid_spec=grid_spec,
        out_shape=jax.ShapeDtypeStruct((B, S, DIFF_WIDTH), F32),
        compiler_params=_cparams(("parallel", "parallel", "arbitrary")),
    )(qi, ki, last, qd, kvd, kvd, kaux, bq, lam_p, subln)


def _merge_kernel(x_ref, oa_ref, ob_ref, gm_ref, wa_ref, wb_ref, wo_ref, h_ref):
    d = x_ref.shape[-1]
    ya = jnp.dot(oa_ref[...].astype(BF16), wa_ref[...], preferred_element_type=F32)
    yb = jnp.dot(ob_ref[...].astype(BF16), wb_ref[...], preferred_element_type=F32)
    y = gm_ref[:, :d] * ya + gm_ref[:, d:] * yb
    h_ref[...] = x_ref[...] + jnp.dot(y.astype(BF16), wo_ref[...], preferred_element_type=F32)


def _merge(x2d, oa, ob, gm, wa, wb, wo, tm):
    T, D = x2d.shape
    row = lambda w: pl.BlockSpec((tm, w), lambda i: (i, 0))
    full = lambda a: pl.BlockSpec(a.shape, lambda i: (0, 0))
    return pl.pallas_call(
        _merge_kernel,
        grid=(T // tm,),
        in_specs=[row(D), row(NSA_WIDTH), row(DIFF_WIDTH), row(2 * D), full(wa), full(wb), full(wo)],
        out_specs=row(D),
        out_shape=jax.ShapeDtypeStruct((T, D), F32),
        compiler_params=_cparams(("parallel",)),
    )(x2d, oa, ob, gm, wa, wb, wo)


def _ffn_kernel(*refs, tiles_per_seq, per_row_state):
    if per_row_state:
        (h_ref, gn_ref, wa_ref, wg_ref, cwa_ref, cwg_ref, cba_ref, cbg_ref, wd_ref, pa_ref, pg_ref,
         y_ref, ua_o, ug_o, xn_s, acc_s) = refs
    else:
        (h_ref, gn_ref, wa_ref, wg_ref, cwa_ref, cwg_ref, cba_ref, cbg_ref, wd_ref,
         y_ref, ua_o, ug_o, xn_s, acc_s, carry_s) = refs
    i, j = pl.program_id(0), pl.program_id(1)
    tm = h_ref.shape[0]

    @pl.when(j == 0)
    def _init():
        x = h_ref[...]
        ms = jnp.mean(x * x, axis=-1, keepdims=True)
        xn_s[...] = (x * lax.rsqrt(ms + RMS_EPS) * gn_ref[...]).astype(BF16)
        acc_s[...] = jnp.zeros(acc_s.shape, F32)

    xn = xn_s[...]
    row = lax.broadcasted_iota(jnp.int32, (tm, 1), 0)

    def conv(w_ref, cw_ref, cb_ref, prev_ref, part, u_out):
        u = jnp.dot(xn, w_ref[...], preferred_element_type=F32)
        cw = cw_ref[...]
        if per_row_state:
            prev = prev_ref[...]
            p2, p1 = prev[:, 0, :], prev[:, 1, :]
            u_out[:, 0, :] = p1
            u_out[:, 1, :] = u
        else:
            @pl.when((i % tiles_per_seq) == 0)
            def _zero_state():
                carry_s[j, part] = jnp.zeros((CONV_W - 1, u.shape[1]), F32)

            carry = carry_s[j, part]
            p1 = jnp.where(row == 0, carry[1:2, :], pltpu.roll(u, 1, 0))
            p2 = jnp.where(row == 0, carry[0:1, :], jnp.where(row == 1, carry[1:2, :], pltpu.roll(u, 2, 0)))
            tail = u[tm - 2:, :]
            carry_s[j, part] = tail
            u_out[...] = tail
        return cb_ref[...] + cw[2:3, :] * u + cw[1:2, :] * p1 + cw[0:1, :] * p2

    a = conv(wa_ref, cwa_ref, cba_ref, None if not per_row_state else pa_ref, 0, ua_o)
    g = conv(wg_ref, cwg_ref, cbg_ref, None if not per_row_state else pg_ref, 1, ug_o)
    act = (g * jax.nn.sigmoid(g) * a).astype(BF16)
    acc_s[...] += jnp.dot(act, wd_ref[...], preferred_element_type=F32)

    @pl.when(j == pl.num_programs(1) - 1)
    def _fin():
        y_ref[...] = h_ref[...] + acc_s[...]


def _ffn(h2d, gn, w_up, conv_w, conv_b, w_down, tm, fc, seq_len=None, state=None):
    T, D = h2d.shape
    dff = w_down.shape[0]
    nf = dff // fc
    per_row = state is not None
    cb2 = conv_b[None, :]
    in_specs = [
        pl.BlockSpec((tm, D), lambda i, j: (i, 0)),
        pl.BlockSpec(gn.shape, lambda i, j: (0, 0)),
        pl.BlockSpec((D, fc), lambda i, j: (0, j)),
        pl.BlockSpec((D, fc), lambda i, j: (0, nf + j)),
        pl.BlockSpec((CONV_W, fc), lambda i, j: (0, j)),
        pl.BlockSpec((CONV_W, fc), lambda i, j: (0, nf + j)),
        pl.BlockSpec((1, fc), lambda i, j: (0, j)),
        pl.BlockSpec((1, fc), lambda i, j: (0, nf + j)),
        pl.BlockSpec((fc, D), lambda i, j: (j, 0)),
    ]
    args = [h2d, gn, w_up, w_up, conv_w, conv_w, cb2, cb2, w_down]
    scratch = [pltpu.VMEM((tm, D), BF16), pltpu.VMEM((tm, D), F32)]
    if per_row:
        in_specs += [pl.BlockSpec((tm, CONV_W - 1, fc), lambda i, j: (i, 0, j)),
                     pl.BlockSpec((tm, CONV_W - 1, fc), lambda i, j: (i, 0, nf + j))]
        args += [state, state]
        n_state, tiles_per_seq = T, 1
        st_spec = pl.BlockSpec((tm, CONV_W - 1, fc), lambda i, j: (i, 0, j))
    else:
        tiles_per_seq = seq_len // tm
        n_state = T // tm
        scratch.append(pltpu.VMEM((nf, 2, CONV_W - 1, fc), F32))
        st_spec = pl.BlockSpec((None, CONV_W - 1, fc), lambda i, j: (i, 0, j))
    y, ua, ug = pl.pallas_call(
        functools.partial(_ffn_kernel, tiles_per_seq=tiles_per_seq, per_row_state=per_row),
        grid=(T // tm, nf),
        in_specs=in_specs,
        out_specs=[pl.BlockSpec((tm, D), lambda i, j: (i, 0)), st_spec, st_spec],
        out_shape=[jax.ShapeDtypeStruct((T, D), F32),
                   jax.ShapeDtypeStruct((n_state, CONV_W - 1, dff), F32),
                   jax.ShapeDtypeStruct((n_state, CONV_W - 1, dff), F32)],
        scratch_shapes=scratch,
        compiler_params=_cparams(("arbitrary", "arbitrary")),
    )(*args)
    state_out = jnp.concatenate([ua, ug], axis=-1)
    if not per_row:
        state_out = state_out[tiles_per_seq - 1::tiles_per_seq]
    return y, state_out


def _rows1(q_row, scale):
    return _to_rows(q_row, scale)


def _slope_col(fn, n, rep):
    return jnp.concatenate([jnp.full((1, 1), fn(h // rep), F32) for h in range(n * rep)], axis=0)


def _nsa_local_sample_kernel(q_ref, gate_ref, cmp_ref, win_ref, kvw_ref, loc_o, mask_o, *, past, w_buf, n_pages):
    nc = (past + 1) // NSA_BLOCK
    ns = -(-(past + 1) // NSA_BLOCK)
    n_sel = min(NSA_TOPK, ns)
    nsp = -(-ns // LANES) * LANES
    bpp = PAGE_SIZE // NSA_BLOCK
    q2 = _rows1(q_ref[...], ATTN_SCALE)
    q2b = q2.astype(BF16)
    gate = gate_ref[...]
    slope = _slope_col(_nsa_slope, NSA_HEADS, 1)

    cmp = cmp_ref[...][:nc]
    kc, vc = cmp[:, :LANES], cmp[:, LANES:]
    bend = lax.broadcasted_iota(jnp.int32, (1, nc), 1) * NSA_BLOCK + (NSA_BLOCK - 1)
    dist_c = past - bend
    valid_c = dist_c >= 0
    p_c = _softmax_rows(_nt(q2, kc, precision=HIGHEST) - slope * dist_c.astype(F32), valid_c)
    o_c = _nn(p_c, vc, precision=HIGHEST)

    lane = lax.broadcasted_iota(jnp.int32, (1, nsp), 1)
    cur = past // NSA_BLOCK
    forced = (lane == 0) | (lane == cur) | (lane == cur - 1)
    ii = lax.broadcasted_iota(jnp.int32, (nsp, nsp), 0)
    jj = lax.broadcasted_iota(jnp.int32, (nsp, nsp), 1)
    pg = lax.broadcasted_iota(jnp.int32, (n_pages, nsp), 0)
    nn = lax.broadcasted_iota(jnp.int32, (n_pages, nsp), 1)
    tok_lo = lax.broadcasted_iota(jnp.int32, (n_pages, PAGE_SIZE), 1) < NSA_BLOCK
    for g in range(NSA_KV_HEADS):
        imp = jnp.sum(p_c[g * NSA_GROUP:(g + 1) * NSA_GROUP], axis=0, keepdims=True)
        imp = jnp.concatenate([imp, jnp.zeros((1, nsp - nc), F32)], axis=1)
        score = jnp.where(forced, FORCE_SCORE, jnp.where(lane > cur, -1.0, imp))
        score = jnp.where(lane < ns, score, -3e38)
        col = jnp.sum(jnp.where(ii == jj, score, 0.0), axis=1, keepdims=True)
        ahead = (col > score) | ((col == score) & (ii < jj))
        cnt = jnp.sum(ahead.astype(jnp.int32), axis=0, keepdims=True)
        sel = ((cnt < n_sel) & (lane < ns)).astype(F32)
        c0 = jnp.sum(jnp.where(nn == pg * bpp, sel, 0.0), axis=1, keepdims=True)
        c1 = jnp.sum(jnp.where(nn == pg * bpp + 1, sel, 0.0), axis=1, keepdims=True)
        mask_o[g] = jnp.where(tok_lo, c0, c1)

    win = win_ref[...]
    kw_t, vw_t = win[:LANES].astype(BF16), win[LANES:].astype(BF16)
    kpos = (past - w_buf) + lax.broadcasted_iota(jnp.int32, (1, w_buf), 1)
    dist_w = past - kpos
    valid_w = (dist_w >= 0) & (dist_w < NSA_WINDOW) & (kpos >= 0)
    s_w = jnp.where(valid_w, _nn(q2b, kw_t) - slope * dist_w.astype(F32), NEG)
    new = kvw_ref[...]
    kn, vn = new[:, :LANES], new[:, LANES:]
    s_n = jnp.sum(q2 * kn, axis=-1, keepdims=True)
    m = jnp.maximum(jnp.max(s_w, axis=-1, keepdims=True), s_n)
    p_w = jnp.where(valid_w, jnp.exp(s_w - m), 0.0)
    p_n = jnp.exp(s_n - m)
    o_w = (_nt(p_w.astype(BF16), vw_t) + p_n * vn) / (jnp.sum(p_w, axis=-1, keepdims=True) + p_n)

    loc_o[...] = _gate_rows(gate, 0, 1) * o_c + _gate_rows(gate, 2, 1) * o_w


def _nsa_local_sample(q, gate, kvcmp, win_state, kvw_new, layer, past):
    DB = q.shape[0]
    nc_all = kvcmp.shape[1]
    w_buf = win_state.shape[3]
    n_pages = past // PAGE_SIZE
    return pl.pallas_call(
        functools.partial(_nsa_local_sample_kernel, past=past, w_buf=w_buf, n_pages=n_pages),
        grid=(DB,),
        in_specs=[
            pl.BlockSpec((None, 1, NSA_WIDTH), lambda b: (b, 0, 0)),
            pl.BlockSpec((None, 1, GATE_PAD), lambda b: (b, 0, 0)),
            pl.BlockSpec((None, nc_all, KV_W), lambda b: (b, 0, 0)),
            pl.BlockSpec((None, None, KV_W, w_buf), lambda b: (layer, b, 0, 0)),
            pl.BlockSpec((None, 1, KV_W), lambda b: (b, 0, 0)),
        ],
        out_specs=[pl.BlockSpec((None, NSA_HEADS, LANES), lambda b: (b, 0, 0)),
                   pl.BlockSpec((None, NSA_KV_HEADS, n_pages, PAGE_SIZE), lambda b: (b, 0, 0, 0))],
        out_shape=[jax.ShapeDtypeStruct((DB, NSA_HEADS, LANES), F32),
                   jax.ShapeDtypeStruct((DB, NSA_KV_HEADS, n_pages, PAGE_SIZE), F32)],
        compiler_params=_cparams(("parallel",)),
    )(q, gate, kvcmp, win_state, kvw_new)


def _nsa_sel_sample_kernel(*refs, ppc, past):
    pt_ref = refs[0]
    pages = refs[1:1 + ppc]
    q_ref, gate_ref, new_ref, mask_ref, loc_ref, o_ref, q2_s, m_s, l_s, acc_s = refs[1 + ppc:]
    del pt_ref
    c = pl.program_id(1)
    slope = _slope_col(_nsa_slope, NSA_HEADS, 1)

    @pl.when(c == 0)
    def _init():
        q2_s[...] = _rows1(q_ref[...], ATTN_SCALE)
        m_s[...] = jnp.full(m_s.shape, NEG, F32)
        l_s[...] = jnp.zeros(l_s.shape, F32)
        acc_s[...] = jnp.zeros(acc_s.shape, F32)

    q2 = q2_s[...]
    q2b = q2.astype(BF16)
    tiles = [r[...] for r in pages]
    ks_t = jnp.concatenate([t[:LANES] for t in tiles], axis=1).astype(BF16)
    vs_t = jnp.concatenate([t[LANES:] for t in tiles], axis=1).astype(BF16)
    n = ppc * PAGE_SIZE
    kpos = c * n + lax.broadcasted_iota(jnp.int32, (1, n), 1)
    dist = (past - kpos).astype(F32)
    mask = mask_ref[...]
    vrow = [jnp.concatenate([mask[g, i:i + 1, :] for i in range(ppc)], axis=1) for g in range(NSA_KV_HEADS)]
    second_group = lax.broadcasted_iota(jnp.int32, (NSA_HEADS, n), 0) >= NSA_GROUP
    valid = jnp.where(second_group, vrow[1], vrow[0]) > 0.5
    s = jnp.where(valid, _nn(q2b, ks_t) - slope * dist, NEG)
    m_old = m_s[...]
    m_new = jnp.maximum(m_old, jnp.max(s, axis=-1, keepdims=True))
    alpha = jnp.exp(m_old - m_new)
    p = jnp.where(valid, jnp.exp(s - m_new), 0.0)
    l_s[...] = alpha * l_s[...] + jnp.sum(p, axis=-1, keepdims=True)
    acc_s[...] = alpha * acc_s[...] + _nt(p.astype(BF16), vs_t)
    m_s[...] = m_new

    @pl.when(c == pl.num_programs(1) - 1)
    def _fin():
        new = new_ref[...]
        kn, vn = new[:, :LANES], new[:, LANES:]
        s_n = jnp.sum(q2 * kn, axis=-1, keepdims=True)
        m_old = m_s[...]
        m_new = jnp.maximum(m_old, s_n)
        alpha = jnp.exp(m_old - m_new)
        p_n = jnp.exp(s_n - m_new)
        o_s = (alpha * acc_s[...] + p_n * vn) / (alpha * l_s[...] + p_n)
        o_ref[...] = _from_rows(loc_ref[...] + _gate_rows(gate_ref[...], 1, 1) * o_s, 1)


def _nsa_sel_sample(pool, page_table, q, gate, kvs_new, mask, loc, layer, past, ppc):
    DB, n_pages = page_table.shape
    grid_spec = pltpu.PrefetchScalarGridSpec(
        num_scalar_prefetch=1,
        grid=(DB, n_pages // ppc),
        in_specs=_page_specs(ppc, layer, KV_W, PAGE_SIZE) + [
            pl.BlockSpec((None, 1, NSA_WIDTH), lambda b, c, pt: (b, 0, 0)),
            pl.BlockSpec((None, 1, GATE_PAD), lambda b, c, pt: (b, 0, 0)),
            pl.BlockSpec((None, 1, KV_W), lambda b, c, pt: (b, 0, 0)),
            pl.BlockSpec((None, NSA_KV_HEADS, ppc, PAGE_SIZE), lambda b, c, pt: (b, 0, c, 0)),
            pl.BlockSpec((None, NSA_HEADS, LANES), lambda b, c, pt: (b, 0, 0)),
        ],
        out_specs=pl.BlockSpec((None, 1, NSA_WIDTH), lambda b, c, pt: (b, 0, 0)),
        scratch_shapes=[pltpu.VMEM((NSA_HEADS, LANES), F32), pltpu.VMEM((NSA_HEADS, 1), F32),
                        pltpu.VMEM((NSA_HEADS, 1), F32), pltpu.VMEM((NSA_HEADS, LANES), F32)],
    )
    return pl.pallas_call(
        functools.partial(_nsa_sel_sample_kernel, ppc=ppc, past=past),
        grid_spec=grid_spec,
        out_shape=jax.ShapeDtypeStruct((DB, 1, NSA_WIDTH), F32),
        compiler_params=_cparams(("parallel", "arbitrary")),
    )(page_table, *([pool] * ppc), q, gate, kvs_new, mask, loc)


def _diff_sample_kernel(*refs, ppc, past, lam_init):
    pages = refs[1:1 + ppc]
    q_ref, new_ref, lam_ref, subln_ref, o_ref, q2_s, m_s, l_s, acc_s = refs[1 + ppc:]
    c = pl.program_id(1)
    rows = 2 * DIFF_HEADS
    slots = 2 * DIFF_HEADS
    slope = _slope_col(_diff_slope, DIFF_HEADS, 2)

    @pl.when(c == 0)
    def _init():
        q = q_ref[...] * ATTN_SCALE
        lo_half = lax.broadcasted_iota(jnp.int32, (1, LANES), 1) < HEAD_DIM
        pieces = []
        for r in range(rows):
            piece = q[:, (r // 2) * LANES:(r // 2 + 1) * LANES]
            pieces.append(jnp.where(lo_half == (r % 2 == 0), piece, 0.0))
        q2_s[...] = jnp.concatenate(pieces, axis=0)
        m_s[...] = jnp.full(m_s.shape, NEG, F32)
        l_s[...] = jnp.zeros(l_s.shape, F32)
        acc_s[...] = jnp.zeros(acc_s.shape, F32)

    q2 = q2_s[...]
    kv = jnp.concatenate([r[...] for r in pages], axis=0).astype(BF16)
    n = ppc * PAGE_SIZE * slots
    lane = lax.broadcasted_iota(jnp.int32, (rows, n), 1)
    head = lax.broadcasted_iota(jnp.int32, (rows, n), 0) // 2
    valid = (lane % slots) == head
    kpos = c * (ppc * PAGE_SIZE) + lane[:1] // slots
    s = jnp.where(valid, _nt(q2.astype(BF16), kv) - slope * (past - kpos).astype(F32), NEG)
    m_old = m_s[...]
    m_new = jnp.maximum(m_old, jnp.max(s, axis=-1, keepdims=True))
    alpha = jnp.exp(m_old - m_new)
    p = jnp.exp(s - m_new)
    l_s[...] = alpha * l_s[...] + jnp.sum(p, axis=-1, keepdims=True)
    acc_s[...] = alpha * acc_s[...] + _nn(pltpu.roll(p, DIFF_HEADS, 1).astype(BF16), kv)
    m_s[...] = m_new

    @pl.when(c == pl.num_programs(1) - 1)
    def _fin():
        new = new_ref[...]
        kn = jnp.concatenate([new[:, (r // 2) * LANES:(r // 2 + 1) * LANES] for r in range(rows)], axis=0)
        vn = jnp.concatenate([new[:, DIFF_WIDTH + (r // 2) * LANES:DIFF_WIDTH + (r // 2 + 1) * LANES]
                              for r in range(rows)], axis=0)
        s_n = jnp.sum(q2 * kn, axis=-1, keepdims=True)
        m_old = m_s[...]
        m_new = jnp.maximum(m_old, s_n)
        alpha = jnp.exp(m_old - m_new)
        p_n = jnp.exp(s_n - m_new)
        o = (alpha * acc_s[...] + p_n * vn) / (alpha * l_s[...] + p_n)
        lam = _lambda_value(lam_ref[...], lam_init)
        outs = [_subln(o[2 * h:2 * h + 1] - lam * o[2 * h + 1:2 * h + 2], subln_ref[...], lam_init)
                for h in range(DIFF_HEADS)]
        o_ref[...] = jnp.concatenate(outs, axis=1)


def _diff_sample(pool, page_table, qd, kvd_new, lam_p, subln, lam_init, layer, past, ppc):
    DB, n_pages = page_table.shape
    grid_spec = pltpu.PrefetchScalarGridSpec(
        num_scalar_prefetch=1,
        grid=(DB, n_pages // ppc),
        in_specs=_page_specs(ppc, layer, PAGE_SIZE * 2 * DIFF_HEADS, DIFF_V_DIM) + [
            pl.BlockSpec((None, 1, DIFF_WIDTH), lambda b, c, pt: (b, 0, 0)),
            pl.BlockSpec((None, 1, 2 * DIFF_WIDTH), lambda b, c, pt: (b, 0, 0)),
            pl.BlockSpec(lam_p.shape, lambda b, c, pt: (0, 0)),
            pl.BlockSpec(subln.shape, lambda b, c, pt: (0, 0)),
        ],
        out_specs=pl.BlockSpec((None, 1, DIFF_WIDTH), lambda b, c, pt: (b, 0, 0)),
        scratch_shapes=[pltpu.VMEM((2 * DIFF_HEADS, DIFF_V_DIM), F32), pltpu.VMEM((2 * DIFF_HEADS, 1), F32),
                        pltpu.VMEM((2 * DIFF_HEADS, 1), F32), pltpu.VMEM((2 * DIFF_HEADS, DIFF_V_DIM), F32)],
    )
    return pl.pallas_call(
        functools.partial(_diff_sample_kernel, ppc=ppc, past=past, lam_init=lam_init),
        grid_spec=grid_spec,
        out_shape=jax.ShapeDtypeStruct((DB, 1, DIFF_WIDTH), F32),
        compiler_params=_cparams(("parallel", "arbitrary")),
    )(page_table, *([pool] * ppc), qd, kvd_new, lam_p, subln)


def _pick(n, prefs):
    for p in prefs:
        if n % p == 0:
            return p
    return n


def _ffn_chunk(dff):
    for k in (2, 1, 4, 11, 22):
        if dff % k == 0 and (dff // k) % LANES == 0 and dff // k <= 1536:
            return dff // k
    return dff


def kernel(x_prompt, x_sample, cache_nsa_cmp, cache_nsa_slc, cache_diff, state_nsa_win, state_ffn_conv,
           page_table, norm_attn, w_in, qk_gain_nsa, qk_gain_diff, nsa_cmp_w, diff_lambda, diff_subln,
           w_br_a, w_br_b, w_o, norm_ffn, w_up, conv_w, conv_b, w_down):
    B, S, D = x_prompt.shape
    DB, t_new, _ = x_sample.shape
    assert t_new == 1, "the sample group carries one new token per sequence"
    depth = w_in.shape[0]
    n_pages = page_table.shape[1]
    past = n_pages * PAGE_SIZE
    n_phys = cache_nsa_cmp.shape[0]
    w_buf = state_nsa_win.shape[2]
    dff = w_down.shape[1]
    assert S % PAGE_SIZE == 0 and S >= NSA_WINDOW + PAGE_SIZE

    tm_proj = _pick(S, (256, 128))
    tm_row = _pick(S, (512, 256, 128))
    tq_nsa = 128
    tk_nsa = _pick(S, (512, 256, 128))
    tq_diff = _pick(S, (512, 256, 128))
    tk_diff = tq_diff
    fc = _ffn_chunk(dff)
    ppc = _pick(n_pages, (8, 4, 2, 1))
    ppc_cmp = _pick(n_pages, (8, 4))

    chan_major = lambda a: jnp.transpose(a, (0, 1, 3, 4, 5, 2)).reshape(a.shape[0], a.shape[1], KV_W, a.shape[2])
    pool_cmp = chan_major(cache_nsa_cmp)
    pool_slc = chan_major(cache_nsa_slc)
    pool_diff = cache_diff.reshape(n_phys, depth, PAGE_SIZE * 2 * DIFF_HEADS, DIFF_V_DIM)
    win_state = chan_major(state_nsa_win)
    pmat = _avg_matrix()
    kaux = _key_aux(S)

    xp = x_prompt.reshape(B * S, D)
    xs = x_sample.reshape(DB, D)
    outs_p = {k: [] for k in ("cmp", "slc", "diff", "win", "conv")}
    outs_s = {k: [] for k in ("cmp", "slc", "diff", "win", "conv")}
    w_keep_p = min(NSA_WINDOW, S)
    w_keep_s = min(NSA_WINDOW, w_buf + 1)

    for l in range(depth):
        lam_init = 0.8 - 0.6 * math.exp(-0.3 * l)
        w_packed = _pack_w_in(w_in[l])
        gains = _pack_gains(qk_gain_nsa[l], qk_gain_diff[l])
        gn_a = norm_attn[l][None, :]
        gn_f = norm_ffn[l][None, :]
        wexp = _cmp_weights(nsa_cmp_w[l])
        w8 = _cmp_page_weights(nsa_cmp_w[l])
        wa, wb, wo = w_br_a[l].astype(BF16), w_br_b[l].astype(BF16), w_o[l].astype(BF16)
        wup, wdn = w_up[l].astype(BF16), w_down[l].astype(BF16)
        lam_p = diff_lambda[l].astype(F32)
        subln = diff_subln[l][None, :]

        q, kvc, kvs, kvw, gate, qd, kvd, gm = _proj(xp, gn_a, w_packed, gains, pmat, tm_proj)
        r3 = lambda a: a.reshape(B, S, a.shape[-1])
        kvcmp = _compress_prompt(r3(kvc), wexp, tk_nsa)
        o_a = _nsa_prompt(r3(q), r3(gate), kvcmp, r3(kvw), r3(kvs), kaux, tq_nsa, tk_nsa)
        o_b = _diff_prompt(r3(qd), r3(kvd), kaux, lam_p, subln, lam_init, tq_diff, tk_diff)
        hp = _merge(xp, o_a.reshape(B * S, -1), o_b.reshape(B * S, -1), gm, wa, wb, wo, tm_row)
        xp, conv_p = _ffn(hp, gn_f, wup, conv_w[l], conv_b[l], wdn, tm_row, fc, seq_len=S)
        outs_p["cmp"].append(kvc.reshape(B, S, 2, NSA_KV_HEADS, HEAD_DIM))
        outs_p["slc"].append(kvs.reshape(B, S, 2, NSA_KV_HEADS, HEAD_DIM))
        outs_p["diff"].append(kvd.reshape(B, S, 2, DIFF_HEADS, DIFF_V_DIM))
        outs_p["win"].append(r3(kvw)[:, S - w_keep_p:].reshape(B, w_keep_p, 2, NSA_KV_HEADS, HEAD_DIM))
        outs_p["conv"].append(conv_p)

        q, kvc, kvs, kvw, gate, qd, kvd, gm = _proj(xs, gn_a, w_packed, gains, pmat, DB)
        r1 = lambda a: a.reshape(DB, 1, a.shape[-1])
        kvcmp = _compress_pages(pool_cmp, page_table, w8, l, ppc_cmp)
        loc, mask = _nsa_local_sample(r1(q), r1(gate), kvcmp, win_state, r1(kvw), l, past)
        o_a = _nsa_sel_sample(pool_slc, page_table, r1(q), r1(gate), r1(kvs), mask, loc, l, past, ppc)
        o_b = _diff_sample(pool_diff, page_table, r1(qd), r1(kvd), lam_p, subln, lam_init, l, past, ppc)
        hs = _merge(xs, o_a.reshape(DB, -1), o_b.reshape(DB, -1), gm, wa, wb, wo, DB)
        xs, conv_s = _ffn(hs, gn_f, wup, conv_w[l], conv_b[l], wdn, DB, fc, state=state_ffn_conv[l])
        outs_s["cmp"].append(kvc.reshape(DB, 1, 2, NSA_KV_HEADS, HEAD_DIM))
        outs_s["slc"].append(kvs.reshape(DB, 1, 2, NSA_KV_HEADS, HEAD_DIM))
        outs_s["diff"].append(kvd.reshape(DB, 1, 2, DIFF_HEADS, DIFF_V_DIM))
        win_all = jnp.concatenate([win_state[l], kvw[:, :, None]], axis=2)[:, :, w_buf + 1 - w_keep_s:]
        outs_s["win"].append(jnp.transpose(win_all.reshape(DB, 2, NSA_KV_HEADS, HEAD_DIM, w_keep_s), (0, 4, 1, 2, 3)))
        outs_s["conv"].append(conv_s)

    return (xp.reshape(B, S, D), xs.reshape(DB, 1, D),
            jnp.stack(outs_p["cmp"], axis=1), jnp.stack(outs_p["slc"], axis=1), jnp.stack(outs_p["diff"], axis=1),
            jnp.stack(outs_p["win"], axis=0), jnp.stack(outs_p["conv"], axis=0),
            jnp.stack(outs_s["cmp"], axis=1), jnp.stack(outs_s["slc"], axis=1), jnp.stack(outs_s["diff"], axis=1),
            jnp.stack(outs_s["win"], axis=0), jnp.stack(outs_s["conv"], axis=0))
```

```python
import functools
import math

import numpy as np
import jax
import jax.numpy as jnp
from jax import lax
from jax.experimental import pallas as pl
from jax.experimental.pallas import tpu as pltpu

HEAD_DIM = 64
NSA_HEADS = 8
NSA_KV_HEADS = 2
NSA_GROUP = NSA_HEADS // NSA_KV_HEADS
NSA_BLOCK = 64
NSA_TOPK = 16
NSA_WINDOW = 512
NSA_WIDTH = NSA_HEADS * HEAD_DIM
DIFF_HEADS = 4
DIFF_V_DIM = 2 * HEAD_DIM
DIFF_WIDTH = DIFF_HEADS * DIFF_V_DIM
CONV_W = 3
PAGE_SIZE = 128
RMS_EPS = 1e-6
ATTN_SCALE = HEAD_DIM ** -0.5
FORCE_SCORE = 1e4
KV_W = 2 * NSA_KV_HEADS * HEAD_DIM
LANES = 128
GATE_PAD = LANES
NEG = -1e30
VMEM_LIMIT = 56 * 1024 * 1024

F32 = jnp.float32
BF16 = jnp.bfloat16
HIGHEST = lax.Precision.HIGHEST

C_Q = 0
C_KVC = C_Q + NSA_WIDTH
C_KVS = C_KVC + KV_W
C_KVW = C_KVS + KV_W
C_GATE = C_KVW + KV_W
C_QD = C_GATE + GATE_PAD
C_KD = C_QD + DIFF_WIDTH
C_VD = C_KD + DIFF_WIDTH
C_GM = C_VD + DIFF_WIDTH


def _nsa_slope(h):
    g, r = h // NSA_GROUP, h % NSA_GROUP
    return 2.0 ** (-8.0 * (r * NSA_KV_HEADS + g + 1) / NSA_HEADS)


def _diff_slope(h):
    return 2.0 ** (-8.0 * (h + 1) / DIFF_HEADS)


LOG2E = math.log2(math.e)
MASK_BIG = 2.0 ** 100
AUX_BIAS = HEAD_DIM


def _split3(x):
    parts, rest = [], np.float64(x)
    for _ in range(3):
        p = np.float64(np.asarray(rest, dtype=np.float32).astype(jnp.bfloat16).astype(np.float32))
        parts.append(float(p))
        rest = rest - p
    return parts


def _bias_q_rows(slopes):
    rows = np.zeros((len(slopes), LANES), np.float32)
    for i, s in enumerate(slopes):
        rows[i, AUX_BIAS:AUX_BIAS + 6] = _split3(s * LOG2E) * 2
    return jnp.asarray(rows)


def _key_aux(n_keys):
    kpos = np.arange(n_keys)
    aux = np.zeros((n_keys, LANES), np.float32)
    blk = kpos // NSA_BLOCK
    ok = blk < AUX_BIAS
    aux[kpos[ok], blk[ok]] = 1.0
    aux[:, AUX_BIAS:AUX_BIAS + 3] = (NSA_BLOCK * blk)[:, None]
    aux[:, AUX_BIAS + 3:AUX_BIAS + 6] = (kpos % NSA_BLOCK)[:, None]
    return jnp.asarray(aux, dtype=BF16)


FLASH_ROWS = 1024


def _flash_update(qc_s, kc, v, m_s, l_s, acc_s, causal=None, groups=1):
    rows = qc_s.shape[0]
    qc, m_all, l_all, acc_all = qc_s[...], m_s[...], l_s[...], acc_s[...]
    m_out, l_out, acc_out = [], [], []
    chunk = min(FLASH_ROWS, rows)
    for r0 in range(0, rows, chunk):
        rs = slice(r0, r0 + chunk)
        s = _nt(qc[rs, :], kc)
        if causal is not None:
            t, n = causal.shape
            if chunk >= t:
                s = jnp.where(causal[None], s.reshape(chunk // t, t, n), NEG).reshape(chunk, n)
            else:
                s = jnp.where(causal[r0 % t:r0 % t + chunk], s, NEG)
        m_old = m_all[rs, :]
        m_new = jnp.maximum(m_old, jnp.max(s, axis=-1, keepdims=True))
        alpha = jnp.exp2(m_old - m_new)
        p = jnp.exp2(s - jnp.concatenate([m_new] * (s.shape[1] // LANES), axis=1))
        l_out.append(alpha * l_all[rs, :] + jnp.sum(p, axis=-1, keepdims=True))
        acc_out.append(alpha * acc_all[rs, :] + _nn(p.astype(BF16), v))
        m_out.append(m_new)
    m_s[...] = jnp.concatenate(m_out, axis=0)
    l_s[...] = jnp.concatenate(l_out, axis=0)
    acc_s[...] = jnp.concatenate(acc_out, axis=0)


def _nt(a, b, **kw):
    return lax.dot_general(a, b, (((1,), (1,)), ((), ())), preferred_element_type=F32, **kw)


def _nn(a, b, **kw):
    return lax.dot_general(a, b, (((1,), (0,)), ((), ())), preferred_element_type=F32, **kw)


def _cparams(sem):
    return pltpu.CompilerParams(dimension_semantics=sem, vmem_limit_bytes=VMEM_LIMIT)


def _proj_kernel(x_ref, gn_ref, w_ref, gains_ref, pmat_ref,
                 q_o, kvc_o, kvs_o, kvw_o, gate_o, qd_o, kvd_o, gm_o):
    x = x_ref[...]
    ms = jnp.mean(x * x, axis=-1, keepdims=True)
    xn = (x * lax.rsqrt(ms + RMS_EPS) * gn_ref[...]).astype(BF16)
    pmat = pmat_ref[...]

    def seg(c0, width):
        return jnp.dot(xn, w_ref[:, c0:c0 + width], preferred_element_type=F32)

    def headnorm(z, gain_row):
        msq = jnp.dot((z * z).astype(BF16), pmat, preferred_element_type=F32)
        return z * lax.rsqrt(msq + RMS_EPS) * gains_ref[gain_row:gain_row + 1, :]

    def normed(c0, width, gain_row):
        z = seg(c0, width)
        return jnp.concatenate([headnorm(z[:, c:c + LANES], gain_row) for c in range(0, width, LANES)], axis=1)

    q_o[...] = normed(C_Q, NSA_WIDTH, 0)
    for c0, out, row in ((C_KVC, kvc_o, 1), (C_KVS, kvs_o, 2), (C_KVW, kvw_o, 3)):
        z = seg(c0, KV_W)
        out[...] = jnp.concatenate([headnorm(z[:, :LANES], row), z[:, LANES:]], axis=1)
    gate_o[...] = jax.nn.sigmoid(seg(C_GATE, GATE_PAD))
    qd_o[...] = normed(C_QD, DIFF_WIDTH, 4)
    kvd_o[:, :DIFF_WIDTH] = normed(C_KD, DIFF_WIDTH, 5)
    kvd_o[:, DIFF_WIDTH:] = seg(C_VD, DIFF_WIDTH)
    d_model = x.shape[-1]
    for c in range(0, 2 * d_model, 512):
        gm_o[:, c:c + 512] = jax.nn.sigmoid(seg(C_GM + c, 512))


def _proj(x2d, gn, w_packed, gains, pmat, tm):
    T, D = x2d.shape
    widths = (NSA_WIDTH, KV_W, KV_W, KV_W, GATE_PAD, DIFF_WIDTH, 2 * DIFF_WIDTH, 2 * D)
    full = lambda a: pl.BlockSpec(a.shape, lambda i: (0, 0))
    return pl.pallas_call(
        _proj_kernel,
        grid=(T // tm,),
        in_specs=[pl.BlockSpec((tm, D), lambda i: (i, 0)), full(gn), full(w_packed), full(gains), full(pmat)],
        out_specs=[pl.BlockSpec((tm, w), lambda i: (i, 0)) for w in widths],
        out_shape=[jax.ShapeDtypeStruct((T, w), F32) for w in widths],
        compiler_params=_cparams(("parallel",)),
    )(x2d, gn, w_packed, gains, pmat)


def _pack_w_in(w_in_l):
    D = w_in_l.shape[0]
    sizes = (NSA_WIDTH, KV_W, KV_W, KV_W, 3 * NSA_HEADS, DIFF_WIDTH, DIFF_WIDTH, DIFF_WIDTH, 2 * D)
    offs = np.cumsum(sizes)[:-1].tolist()
    q, kvc, kvs, kvw, gn, qd, kd, vd, gm = jnp.split(w_in_l, offs, axis=1)
    gn = jnp.pad(gn, ((0, 0), (0, GATE_PAD - gn.shape[1])))
    perm = lambda w: w.reshape(D, 2, DIFF_HEADS, HEAD_DIM).transpose(0, 2, 1, 3).reshape(D, DIFF_WIDTH)
    return jnp.concatenate([q, kvc, kvs, kvw, gn, perm(qd), perm(kd), vd, gm], axis=1).astype(BF16)


def _pack_gains(g_nsa, g_diff):
    two = lambda a, b: jnp.concatenate([a, b])[None, :]
    rows = [two(g_nsa[0], g_nsa[0]), two(g_nsa[1], g_nsa[1]), two(g_nsa[2], g_nsa[2]), two(g_nsa[3], g_nsa[3]),
            two(g_diff[0], g_diff[1]), two(g_diff[2], g_diff[3])]
    rows += [jnp.ones((1, LANES), F32)] * 2
    return jnp.concatenate(rows, axis=0).astype(F32)


def _avg_matrix():
    i = np.arange(LANES)
    return jnp.asarray((i[:, None] // HEAD_DIM == i[None, :] // HEAD_DIM) / HEAD_DIM, dtype=BF16)


def _compress_kernel(*refs, n_in, n_prefetch):
    refs = refs[n_prefetch:]
    w_ref, out_ref = refs[n_in], refs[n_in + 1]
    w = w_ref[...]
    outs = []
    for r in refs[:n_in]:
        x = r[...].reshape(-1, NSA_BLOCK, KV_W)
        outs.append(jnp.sum(x * w[None], axis=1))
    out_ref[...] = jnp.concatenate(outs, axis=0).reshape(out_ref.shape)


def _cmp_weights(cmp_w_l):
    return jnp.concatenate([jnp.broadcast_to(cmp_w_l[0][:, None], (NSA_BLOCK, KV_W // 2)),
                            jnp.broadcast_to(cmp_w_l[1][:, None], (NSA_BLOCK, KV_W // 2))], axis=1).astype(F32)


def _compress_prompt(kvc, wexp, tk):
    B, S, _ = kvc.shape
    nb = tk // NSA_BLOCK
    return pl.pallas_call(
        functools.partial(_compress_kernel, n_in=1, n_prefetch=0),
        grid=(B, S // tk),
        in_specs=[pl.BlockSpec((None, tk, KV_W), lambda b, i: (b, i, 0)),
                  pl.BlockSpec(wexp.shape, lambda b, i: (0, 0))],
        out_specs=pl.BlockSpec((None, nb, KV_W), lambda b, i: (b, i, 0)),
        out_shape=jax.ShapeDtypeStruct((B, S // NSA_BLOCK, KV_W), F32),
        compiler_params=_cparams(("parallel", "parallel")),
    )(kvc, wexp)


def _page_specs(n, layer, rows, cols):
    def mk(i):
        return pl.BlockSpec((None, None, rows, cols), lambda b, c, pt: (pt[b, c * n + i], layer, 0, 0))
    return [mk(i) for i in range(n)]


def _split_bf16(x):
    hi = x.astype(BF16)
    return hi, (x - hi.astype(F32)).astype(BF16)


def _compress_pages_kernel(*refs, ppc):
    pages, w_ref, out_ref = refs[1:1 + ppc], refs[1 + ppc], refs[2 + ppc]
    w_hi, w_lo = _split_bf16(w_ref[...])
    w_both = jnp.concatenate([w_hi, w_lo], axis=0)
    k_cols = lax.broadcasted_iota(jnp.int32, (2, KV_W), 1) < KV_W // 2
    outs = []
    for r in pages:
        hi, lo = _split_bf16(r[...])
        o16 = _nt(w_both, hi)
        o8 = o16[:8] + o16[8:] + _nt(w_hi, lo)
        outs.append(jnp.where(k_cols, o8[0:2], o8[2:4]))
    out_ref[...] = jnp.concatenate(outs, axis=0)


def _cmp_page_weights(cmp_w_l):
    z = jnp.zeros((NSA_BLOCK,), F32)
    rows = [jnp.concatenate([cmp_w_l[0], z]), jnp.concatenate([z, cmp_w_l[0]]),
            jnp.concatenate([cmp_w_l[1], z]), jnp.concatenate([z, cmp_w_l[1]])]
    return jnp.concatenate([jnp.stack(rows), jnp.zeros((4, PAGE_SIZE), F32)], axis=0).astype(F32)


def _compress_pages(pool_t, page_table, w8, layer, ppc):
    DB, n_pages = page_table.shape
    bpp = PAGE_SIZE // NSA_BLOCK
    grid_spec = pltpu.PrefetchScalarGridSpec(
        num_scalar_prefetch=1,
        grid=(DB, n_pages // ppc),
        in_specs=_page_specs(ppc, layer, KV_W, PAGE_SIZE) + [pl.BlockSpec(w8.shape, lambda b, c, pt: (0, 0))],
        out_specs=pl.BlockSpec((None, ppc * bpp, KV_W), lambda b, c, pt: (b, c, 0)),
    )
    return pl.pallas_call(
        functools.partial(_compress_pages_kernel, ppc=ppc),
        grid_spec=grid_spec,
        out_shape=jax.ShapeDtypeStruct((DB, n_pages * bpp, KV_W), F32),
        compiler_params=_cparams(("parallel", "parallel")),
    )(page_table, *([pool_t] * ppc), w8)


def _to_rows(q, scale):
    t = q.shape[0]
    hi_half = lax.broadcasted_iota(jnp.int32, (t, LANES), 1) >= HEAD_DIM
    pieces = []
    for h in range(NSA_HEADS):
        g = h // NSA_GROUP
        c = q[:, (h // 2) * LANES:(h // 2 + 1) * LANES]
        if (h % 2) != g:
            c = pltpu.roll(c, HEAD_DIM, 1)
        keep = hi_half if g == 1 else jnp.logical_not(hi_half)
        pieces.append(jnp.where(keep, c * scale, 0.0))
    return jnp.concatenate(pieces, axis=0)


def _from_rows(o, t):
    lo_half = lax.broadcasted_iota(jnp.int32, (t, LANES), 1) < HEAD_DIM
    chunks = []
    for j in range(NSA_HEADS // 2):
        a = o[(2 * j) * t:(2 * j + 1) * t]
        b = o[(2 * j + 1) * t:(2 * j + 2) * t]
        if j // 2 == 0:
            chunks.append(jnp.where(lo_half, a, pltpu.roll(b, HEAD_DIM, 1)))
        else:
            chunks.append(jnp.where(lo_half, pltpu.roll(a, HEAD_DIM, 1), b))
    return jnp.concatenate(chunks, axis=1)


def _gate_rows(gate, branch, t):
    cols = [jnp.broadcast_to(gate[:, branch * NSA_HEADS + h:branch * NSA_HEADS + h + 1], (t, LANES))
            for h in range(NSA_HEADS)]
    return jnp.concatenate(cols, axis=0)


def _softmax_rows(s, valid):
    m = jnp.max(jnp.where(valid, s, NEG), axis=-1, keepdims=True)
    p = jnp.where(valid, jnp.exp(s - m), 0.0)
    return p / jnp.maximum(jnp.sum(p, axis=-1, keepdims=True), 1e-30)


def _nsa_prompt_kernel(qi_tab, ki_tab, last_tab, q_ref, gate_ref, cmp_ref, kvw_ref, kaux_all_ref, bq_ref,
                       kvs_ref, kaux_ref, o_ref, qc_s, loc_s, g1_s, m_s, l_s, acc_s, *, tq, tk, seq):
    step = pl.program_id(1)
    qi, ki = qi_tab[step], ki_tab[step]
    q0 = qi * tq
    nc = seq // NSA_BLOCK
    ns = -(-seq // NSA_BLOCK)
    n_sel = min(NSA_TOPK, ns)
    wk = NSA_WINDOW + tq

    @pl.when(ki == 0)
    def _init():
        q2 = _to_rows(q_ref[...], ATTN_SCALE)
        gate = gate_ref[...]
        qpos_col = q0 + lax.broadcasted_iota(jnp.int32, (tq, 1), 0)
        qpos_row = q0 + lax.broadcasted_iota(jnp.int32, (1, tq), 1)

        cmp = cmp_ref[...]
        kc, vc = cmp[:, :LANES], cmp[:, LANES:]
        bend_row = lax.broadcasted_iota(jnp.int32, (1, nc), 1) * NSA_BLOCK + (NSA_BLOCK - 1)
        dist_c = qpos_col - bend_row
        valid_c = dist_c >= 0
        s_c = _nt(q2.astype(BF16), kc.astype(BF16))
        vcb = vc.astype(BF16)
        o_c = []
        for h in range(NSA_HEADS):
            sh = s_c[h * tq:(h + 1) * tq] - _nsa_slope(h) * dist_c.astype(F32)
            o_c.append(_nn(_softmax_rows(sh, valid_c).astype(BF16), vcb))
        o_c = jnp.concatenate(o_c, axis=0)

        bend_col = lax.broadcasted_iota(jnp.int32, (nc, 1), 0) * NSA_BLOCK + (NSA_BLOCK - 1)
        dist_t = qpos_row - bend_col
        valid_t = dist_t >= 0
        s_t = _nt(kc, q2, precision=HIGHEST)
        blk = lax.broadcasted_iota(jnp.int32, (ns, tq), 0)
        cur = qpos_row // NSA_BLOCK
        sel_t = []
        for g in range(NSA_KV_HEADS):
            imp = jnp.zeros((nc, tq), F32)
            for r in range(NSA_GROUP):
                h = g * NSA_GROUP + r
                sh = s_t[:, h * tq:(h + 1) * tq] - _nsa_slope(h) * dist_t.astype(F32)
                m = jnp.max(jnp.where(valid_t, sh, NEG), axis=0, keepdims=True)
                p = jnp.where(valid_t, jnp.exp(sh - m), 0.0)
                imp = imp + p / jnp.maximum(jnp.sum(p, axis=0, keepdims=True), 1e-30)
            if ns > nc:
                imp = jnp.concatenate([imp, jnp.zeros((ns - nc, tq), F32)], axis=0)
            forced = (blk == 0) | (blk == cur) | (blk == cur - 1)
            score = jnp.where(forced, FORCE_SCORE, jnp.where(blk > cur, -1.0, imp))
            grp = [score[j:j + 8] for j in range(0, ns, 8)]
            below = lax.broadcasted_iota(jnp.int32, (8, tq), 0)
            cnt = [jnp.zeros((8, tq), jnp.int32) for _ in grp]
            for i in range(ns):
                row = jnp.broadcast_to(score[i:i + 1, :], (8, tq))
                for j, sc in enumerate(grp):
                    if 8 * j > i:
                        ahead = row >= sc
                    elif 8 * j + 7 <= i:
                        ahead = row > sc
                    else:
                        ahead = (row > sc) | ((row == sc) & (below > i - 8 * j))
                    cnt[j] = cnt[j] + ahead.astype(jnp.int32)
            sel_t.append((jnp.concatenate(cnt, axis=0) < n_sel).astype(F32))
        sel = jnp.concatenate(sel_t, axis=0).T

        q2l = (q2 * LOG2E).astype(BF16)
        low = lax.broadcasted_iota(jnp.int32, (tq, LANES), 1) < AUX_BIAS
        pad = jnp.ones((tq, LANES - ns), F32)
        aux_w, aux_s = [], []
        for h in range(NSA_HEADS):
            g = h // NSA_GROUP
            bq = jnp.broadcast_to(bq_ref[h:h + 1, :], (tq, LANES))
            drop = (jnp.concatenate([sel[:, g * ns:(g + 1) * ns], pad], axis=1) - 1.0) * MASK_BIG
            aux_w.append(jnp.where(low, 0.0, bq))
            aux_s.append(jnp.where(low, drop, bq))
        qc_w = jnp.concatenate([q2l, jnp.concatenate(aux_w, axis=0).astype(BF16)], axis=1)
        qc_s[...] = jnp.concatenate([q2l, jnp.concatenate(aux_s, axis=0).astype(BF16)], axis=1)

        start = pl.multiple_of(jnp.clip(q0 - NSA_WINDOW, 0, seq - wk), LANES)
        kvw = kvw_ref[pl.ds(start, wk), :]
        kcw = jnp.concatenate([kvw[:, :LANES].astype(BF16), kaux_all_ref[pl.ds(start, wk), :]], axis=1)
        vw = kvw[:, LANES:].astype(BF16)
        kpos = start + lax.broadcasted_iota(jnp.int32, (1, wk), 1)
        dist_w = qpos_col - kpos
        valid_w = ((dist_w >= 0) & (dist_w < NSA_WINDOW))[None]
        s_w = jnp.where(valid_w, _nt(qc_w, kcw).reshape(NSA_HEADS, tq, wk), NEG).reshape(NSA_HEADS * tq, wk)
        p_w = jnp.exp2(s_w - jnp.max(s_w, axis=-1, keepdims=True))
        o_w = _nn(p_w.astype(BF16), vw) / jnp.sum(p_w, axis=-1, keepdims=True)

        loc_s[...] = _gate_rows(gate, 0, tq) * o_c + _gate_rows(gate, 2, tq) * o_w
        g1_s[...] = _gate_rows(gate, 1, tq)
        m_s[...] = jnp.full(m_s.shape, NEG, F32)
        l_s[...] = jnp.zeros(l_s.shape, F32)
        acc_s[...] = jnp.zeros(acc_s.shape, F32)

    kvs = kvs_ref[...]
    kc_t = jnp.concatenate([kvs[:, :LANES].astype(BF16), kaux_ref[...]], axis=1)
    vs = kvs[:, LANES:].astype(BF16)
    is_last = last_tab[step] == 1

    @pl.when(jnp.logical_not(is_last))
    def _interior():
        _flash_update(qc_s, kc_t, vs, m_s, l_s, acc_s)

    @pl.when(is_last)
    def _diag():
        kpos = ki * tk + lax.broadcasted_iota(jnp.int32, (1, tk), 1)
        qpos = q0 + lax.broadcasted_iota(jnp.int32, (tq, 1), 0)
        _flash_update(qc_s, kc_t, vs, m_s, l_s, acc_s, causal=kpos <= qpos)
        o_ref[...] = _from_rows(loc_s[...] + g1_s[...] * (acc_s[...] / l_s[...]), tq)


def _causal_pairs(seq, tq, tk):
    assert tk % tq == 0, "only the last key tile of a query tile may cross the diagonal"
    qi, ki, last = [], [], []
    for i in range(seq // tq):
        n = (i * tq + tq - 1) // tk + 1
        for j in range(n):
            qi.append(i), ki.append(j), last.append(1 if j == n - 1 else 0)
    mk = lambda a: jnp.asarray(np.asarray(a, dtype=np.int32))
    return mk(qi), mk(ki), mk(last)


def _nsa_prompt(q, gate, kvcmp, kvw, kvs, kaux, tq, tk):
    B, S, _ = q.shape
    nc = S // NSA_BLOCK
    ns = -(-S // NSA_BLOCK)
    assert ns <= AUX_BIAS, "the block one-hot must fit below the ALiBi lanes"
    qi, ki, last = _causal_pairs(S, tq, tk)
    bq = _bias_q_rows([_nsa_slope(h) for h in range(NSA_HEADS)])
    rows = NSA_HEADS * tq
    grid_spec = pltpu.PrefetchScalarGridSpec(
        num_scalar_prefetch=3,
        grid=(B, int(qi.shape[0])),
        in_specs=[
            pl.BlockSpec((None, tq, NSA_WIDTH), lambda b, s, qt, kt, lt: (b, qt[s], 0)),
            pl.BlockSpec((None, tq, GATE_PAD), lambda b, s, qt, kt, lt: (b, qt[s], 0)),
            pl.BlockSpec((None, nc, KV_W), lambda b, s, qt, kt, lt: (b, 0, 0)),
            pl.BlockSpec((None, S, KV_W), lambda b, s, qt, kt, lt: (b, 0, 0)),
            pl.BlockSpec((S, LANES), lambda b, s, qt, kt, lt: (0, 0)),
            pl.BlockSpec(bq.shape, lambda b, s, qt, kt, lt: (0, 0)),
            pl.BlockSpec((None, tk, KV_W), lambda b, s, qt, kt, lt: (b, kt[s], 0)),
            pl.BlockSpec((tk, LANES), lambda b, s, qt, kt, lt: (kt[s], 0)),
        ],
        out_specs=pl.BlockSpec((None, tq, NSA_WIDTH), lambda b, s, qt, kt, lt: (b, qt[s], 0)),
        scratch_shapes=[
            pltpu.VMEM((rows, 2 * LANES), BF16),
            pltpu.VMEM((rows, LANES), F32),
            pltpu.VMEM((rows, LANES), F32),
            pltpu.VMEM((rows, LANES), F32),
            pltpu.VMEM((rows, LANES), F32),
            pltpu.VMEM((rows, LANES), F32),
        ],
    )
    return pl.pallas_call(
        functools.partial(_nsa_prompt_kernel, tq=tq, tk=tk, seq=S),
        grid_spec=grid_spec,
        out_shape=jax.ShapeDtypeStruct((B, S, NSA_WIDTH), F32),
        compiler_params=_cparams(("parallel", "arbitrary")),
    )(qi, ki, last, q, gate, kvcmp, kvw, kaux, bq, kvs, kaux)


def _lambda_value(lp, lam_init):
    a = jnp.sum(lp[0:1] * lp[1:2], axis=-1, keepdims=True)
    b = jnp.sum(lp[2:3] * lp[3:4], axis=-1, keepdims=True)
    return jnp.exp(a) - jnp.exp(b) + lam_init


def _subln(o, g, lam_init):
    ms = jnp.mean(o * o, axis=-1, keepdims=True)
    return o * lax.rsqrt(ms + RMS_EPS) * g * (1.0 - lam_init)


def _diff_prompt_kernel(qi_tab, ki_tab, last_tab, q_ref, k_ref, v_ref, kaux_ref, bq_ref, lam_ref, subln_ref, o_ref,
                        qc_s, m_s, l_s, acc_s, *, tq, tk, lam_init):
    step = pl.program_id(2)
    qi, ki = qi_tab[step], ki_tab[step]

    @pl.when(ki == 0)
    def _init():
        q = q_ref[...] * (ATTN_SCALE * LOG2E)
        lo_half = lax.broadcasted_iota(jnp.int32, (tq, LANES), 1) < HEAD_DIM
        q2 = jnp.concatenate([jnp.where(lo_half, q, 0.0), jnp.where(lo_half, 0.0, q)], axis=0)
        aux = jnp.broadcast_to(bq_ref[0:1, :], (2 * tq, LANES))
        qc_s[...] = jnp.concatenate([q2, aux], axis=1).astype(BF16)
        m_s[...] = jnp.full(m_s.shape, NEG, F32)
        l_s[...] = jnp.zeros(l_s.shape, F32)
        acc_s[...] = jnp.zeros(acc_s.shape, F32)

    kc = jnp.concatenate([k_ref[...].astype(BF16), kaux_ref[...]], axis=1)
    v = v_ref[...].astype(BF16)
    is_last = last_tab[step] == 1

    @pl.when(jnp.logical_not(is_last))
    def _interior():
        _flash_update(qc_s, kc, v, m_s, l_s, acc_s)

    @pl.when(is_last)
    def _diag():
        kpos = ki * tk + lax.broadcasted_iota(jnp.int32, (1, tk), 1)
        qpos = qi * tq + lax.broadcasted_iota(jnp.int32, (tq, 1), 0)
        _flash_update(qc_s, kc, v, m_s, l_s, acc_s, causal=kpos <= qpos)
        o = acc_s[...] / l_s[...]
        lam = _lambda_value(lam_ref[...], lam_init)
        o_ref[...] = _subln(o[:tq] - lam * o[tq:], subln_ref[...], lam_init)


def _diff_prompt(qd, kvd, kaux, lam_p, subln, lam_init, tq, tk):
    B, S, _ = qd.shape
    qi, ki, last = _causal_pairs(S, tq, tk)
    bq = jnp.repeat(_bias_q_rows([_diff_slope(h) for h in range(DIFF_HEADS)])[:, None, :], 8, axis=1)
    grid_spec = pltpu.PrefetchScalarGridSpec(
        num_scalar_prefetch=3,
        grid=(B, DIFF_HEADS, int(qi.shape[0])),
        in_specs=[
            pl.BlockSpec((None, tq, DIFF_V_DIM), lambda b, h, s, qt, kt, lt: (b, qt[s], h)),
            pl.BlockSpec((None, tk, DIFF_V_DIM), lambda b, h, s, qt, kt, lt: (b, kt[s], h)),
            pl.BlockSpec((None, tk, DIFF_V_DIM), lambda b, h, s, qt, kt, lt: (b, kt[s], DIFF_HEADS + h)),
            pl.BlockSpec((tk, LANES), lambda b, h, s, qt, kt, lt: (kt[s], 0)),
            pl.BlockSpec((None, 8, LANES), lambda b, h, s, qt, kt, lt: (h, 0, 0)),
            pl.BlockSpec(lam_p.shape, lambda b, h, s, qt, kt, lt: (0, 0)),
            pl.BlockSpec(subln.shape, lambda b, h, s, qt, kt, lt: (0, 0)),
        ],
        out_specs=pl.BlockSpec((None, tq, DIFF_V_DIM), lambda b, h, s, qt, kt, lt: (b, qt[s], h)),
        scratch_shapes=[
            pltpu.VMEM((2 * tq, 2 * LANES), BF16),
            pltpu.VMEM((2 * tq, LANES), F32),
            pltpu.VMEM((2 * tq, LANES), F32),
            pltpu.VMEM((2 * tq, DIFF_V_DIM), F32),
        ],
    )
    return pl.pallas_call(
        functools.partial(_diff_prompt_kernel, tq=tq, tk=tk, lam_init=lam_init),
        grid_spec=grid_spec,
        out_shape=jax.ShapeDtypeStruct((B, S, DIFF_WIDTH), F32),
        compiler_params=_cparams(("parallel", "parallel", "arbitrary")),
    )(qi, ki, last, qd, kvd, kvd, kaux, bq, lam_p, subln)


def _merge_kernel(x_ref, oa_ref, ob_ref, gm_ref, wa_ref, wb_ref, wo_ref, h_ref):
    d = x_ref.shape[-1]
    ya = jnp.dot(oa_ref[...].astype(BF16), wa_ref[...], preferred_element_type=F32)
    yb = jnp.dot(ob_ref[...].astype(BF16), wb_ref[...], preferred_element_type=F32)
    y = gm_ref[:, :d] * ya + gm_ref[:, d:] * yb
    h_ref[...] = x_ref[...] + jnp.dot(y.astype(BF16), wo_ref[...], preferred_element_type=F32)


def _merge(x2d, oa, ob, gm, wa, wb, wo, tm):
    T, D = x2d.shape
    row = lambda w: pl.BlockSpec((tm, w), lambda i: (i, 0))
    full = lambda a: pl.BlockSpec(a.shape, lambda i: (0, 0))
    return pl.pallas_call(
        _merge_kernel,
        grid=(T // tm,),
        in_specs=[row(D), row(NSA_WIDTH), row(DIFF_WIDTH), row(2 * D), full(wa), full(wb), full(wo)],
        out_specs=row(D),
        out_shape=jax.ShapeDtypeStruct((T, D), F32),
        compiler_params=_cparams(("parallel",)),
    )(x2d, oa, ob, gm, wa, wb, wo)


def _ffn_kernel(*refs, tiles_per_seq, per_row_state):
    if per_row_state:
        (h_ref, gn_ref, wa_ref, wg_ref, cwa_ref, cwg_ref, cba_ref, cbg_ref, wd_ref, pa_ref, pg_ref,
         y_ref, ua_o, ug_o, xn_s, acc_s) = refs
    else:
        (h_ref, gn_ref, wa_ref, wg_ref, cwa_ref, cwg_ref, cba_ref, cbg_ref, wd_ref,
         y_ref, ua_o, ug_o, xn_s, acc_s, carry_s) = refs
    i, j = pl.program_id(0), pl.program_id(1)
    tm = h_ref.shape[0]

    @pl.when(j == 0)
    def _init():
        x = h_ref[...]
        ms = jnp.mean(x * x, axis=-1, keepdims=True)
        xn_s[...] = (x * lax.rsqrt(ms + RMS_EPS) * gn_ref[...]).astype(BF16)
        acc_s[...] = jnp.zeros(acc_s.shape, F32)

    if not per_row_state:
        @pl.when((i % tiles_per_seq) == 0)
        def _zero_state():
            carry_s[j] = jnp.zeros(carry_s.shape[1:], F32)

    xn = xn_s[...]
    row = lax.broadcasted_iota(jnp.int32, (8, 1), 0)

    def conv(w_ref, cw_ref, cb_ref, prev_ref, part, u_out):
        u = jnp.dot(xn, w_ref[...], preferred_element_type=F32)
        cw = cw_ref[...]
        if per_row_state:
            prev = prev_ref[...]
            p2, p1 = prev[:, 0, :], prev[:, 1, :]
            u_out[:, 0, :] = p1
            u_out[:, 1, :] = u
        else:
            carry = carry_s[j, part]
            r1, r2 = pltpu.roll(u, 1, 0), pltpu.roll(u, 2, 0)
            h1 = jnp.where(row == 0, carry[1:2, :], r1[:8])
            h2 = jnp.where(row == 0, carry[0:1, :], jnp.where(row == 1, carry[1:2, :], r2[:8]))
            p1 = jnp.concatenate([h1, r1[8:]], axis=0)
            p2 = jnp.concatenate([h2, r2[8:]], axis=0)
            tail = u[tm - 2:, :]
            carry_s[j, part] = tail
            u_out[...] = tail
        return cb_ref[...] + cw[2:3, :] * u + cw[1:2, :] * p1 + cw[0:1, :] * p2

    a = conv(wa_ref, cwa_ref, cba_ref, None if not per_row_state else pa_ref, 0, ua_o)
    g = conv(wg_ref, cwg_ref, cbg_ref, None if not per_row_state else pg_ref, 1, ug_o)
    act = (g * jax.nn.sigmoid(g) * a).astype(BF16)
    acc_s[...] += jnp.dot(act, wd_ref[...], preferred_element_type=F32)

    @pl.when(j == pl.num_programs(1) - 1)
    def _fin():
        y_ref[...] = h_ref[...] + acc_s[...]


def _ffn(h2d, gn, w_up, conv_w, conv_b, w_down, tm, fc, seq_len=None, state=None):
    T, D = h2d.shape
    dff = w_down.shape[0]
    nf = dff // fc
    per_row = state is not None
    cb2 = conv_b[None, :]
    in_specs = [
        pl.BlockSpec((tm, D), lambda i, j: (i, 0)),
        pl.BlockSpec(gn.shape, lambda i, j: (0, 0)),
        pl.BlockSpec((D, fc), lambda i, j: (0, j)),
        pl.BlockSpec((D, fc), lambda i, j: (0, nf + j)),
        pl.BlockSpec((CONV_W, fc), lambda i, j: (0, j)),
        pl.BlockSpec((CONV_W, fc), lambda i, j: (0, nf + j)),
        pl.BlockSpec((1, fc), lambda i, j: (0, j)),
        pl.BlockSpec((1, fc), lambda i, j: (0, nf + j)),
        pl.BlockSpec((fc, D), lambda i, j: (j, 0)),
    ]
    args = [h2d, gn, w_up, w_up, conv_w, conv_w, cb2, cb2, w_down]
    scratch = [pltpu.VMEM((tm, D), BF16), pltpu.VMEM((tm, D), F32)]
    if per_row:
        in_specs += [pl.BlockSpec((tm, CONV_W - 1, fc), lambda i, j: (i, 0, j)),
                     pl.BlockSpec((tm, CONV_W - 1, fc), lambda i, j: (i, 0, nf + j))]
        args += [state, state]
        n_state, tiles_per_seq = T, 1
        st_spec = pl.BlockSpec((tm, CONV_W - 1, fc), lambda i, j: (i, 0, j))
    else:
        tiles_per_seq = seq_len // tm
        n_state = T // tm
        scratch.append(pltpu.VMEM((nf, 2, CONV_W - 1, fc), F32))
        st_spec = pl.BlockSpec((None, CONV_W - 1, fc), lambda i, j: (i, 0, j))
    y, ua, ug = pl.pallas_call(
        functools.partial(_ffn_kernel, tiles_per_seq=tiles_per_seq, per_row_state=per_row),
        grid=(T // tm, nf),
        in_specs=in_specs,
        out_specs=[pl.BlockSpec((tm, D), lambda i, j: (i, 0)), st_spec, st_spec],
        out_shape=[jax.ShapeDtypeStruct((T, D), F32),
                   jax.ShapeDtypeStruct((n_state, CONV_W - 1, dff), F32),
                   jax.ShapeDtypeStruct((n_state, CONV_W - 1, dff), F32)],
        scratch_shapes=scratch,
        compiler_params=_cparams(("arbitrary", "arbitrary")),
    )(*args)
    state_out = jnp.concatenate([ua, ug], axis=-1)
    if not per_row:
        state_out = state_out[tiles_per_seq - 1::tiles_per_seq]
    return y, state_out


def _rows1(q_row, scale):
    return _to_rows(q_row, scale)


def _slope_col(fn, n, rep):
    return jnp.concatenate([jnp.full((1, 1), fn(h // rep), F32) for h in range(n * rep)], axis=0)


def _nsa_local_sample_kernel(q_ref, gate_ref, cmp_ref, win_ref, kvw_ref, loc_o, mask_o, list_o,
                             *, past, w_buf, n_pages, n_list):
    nc = (past + 1) // NSA_BLOCK
    ns = -(-(past + 1) // NSA_BLOCK)
    n_sel = min(NSA_TOPK, ns)
    nsp = -(-ns // LANES) * LANES
    bpp = PAGE_SIZE // NSA_BLOCK
    q2 = _rows1(q_ref[...], ATTN_SCALE)
    q2b = q2.astype(BF16)
    gate = gate_ref[...]
    slope = _slope_col(_nsa_slope, NSA_HEADS, 1)

    cmp = cmp_ref[...][:nc]
    kc, vc = cmp[:, :LANES], cmp[:, LANES:]
    bend = lax.broadcasted_iota(jnp.int32, (1, nc), 1) * NSA_BLOCK + (NSA_BLOCK - 1)
    dist_c = past - bend
    valid_c = dist_c >= 0
    p_c = _softmax_rows(_nt(q2, kc, precision=HIGHEST) - slope * dist_c.astype(F32), valid_c)
    o_c = _nn(p_c, vc, precision=HIGHEST)

    lane = lax.broadcasted_iota(jnp.int32, (1, nsp), 1)
    cur = past // NSA_BLOCK
    forced = (lane == 0) | (lane == cur) | (lane == cur - 1)
    ii = lax.broadcasted_iota(jnp.int32, (nsp, nsp), 0)
    jj = lax.broadcasted_iota(jnp.int32, (nsp, nsp), 1)
    pg = lax.broadcasted_iota(jnp.int32, (n_pages, nsp), 0)
    nn = lax.broadcasted_iota(jnp.int32, (n_pages, nsp), 1)
    tok_lo = lax.broadcasted_iota(jnp.int32, (n_pages, PAGE_SIZE), 1) < NSA_BLOCK
    masks, need = [], jnp.zeros((n_pages, 1), F32)
    for g in range(NSA_KV_HEADS):
        imp = jnp.sum(p_c[g * NSA_GROUP:(g + 1) * NSA_GROUP], axis=0, keepdims=True)
        imp = jnp.concatenate([imp, jnp.zeros((1, nsp - nc), F32)], axis=1)
        score = jnp.where(forced, FORCE_SCORE, jnp.where(lane > cur, -1.0, imp))
        score = jnp.where(lane < ns, score, -3e38)
        col = jnp.sum(jnp.where(ii == jj, score, 0.0), axis=1, keepdims=True)
        ahead = (col > score) | ((col == score) & (ii < jj))
        cnt = jnp.sum(ahead.astype(jnp.int32), axis=0, keepdims=True)
        sel = ((cnt < n_sel) & (lane < ns)).astype(F32)
        c0 = jnp.sum(jnp.where(nn == pg * bpp, sel, 0.0), axis=1, keepdims=True)
        c1 = jnp.sum(jnp.where(nn == pg * bpp + 1, sel, 0.0), axis=1, keepdims=True)
        masks.append(jnp.where(tok_lo, c0, c1))
        need = need + c0 + c1

    pi = lax.broadcasted_iota(jnp.int32, (n_pages, n_pages), 0)
    pj = lax.broadcasted_iota(jnp.int32, (n_pages, n_pages), 1)
    needed = (need > 0.5).astype(F32)
    need_row = jnp.sum(jnp.where(pi == pj, needed, 0.0), axis=0, keepdims=True)
    before = jnp.sum(jnp.where(pi < pj, needed, 0.0), axis=0, keepdims=True)
    slot = lax.broadcasted_iota(jnp.int32, (n_list, n_pages), 0).astype(F32)
    onehot = (slot == before) & (need_row > 0.5)
    page = lax.broadcasted_iota(jnp.int32, (n_list, n_pages), 1).astype(F32)
    list_o[...] = jnp.sum(jnp.where(onehot, page, 0.0), axis=1, keepdims=True).astype(jnp.int32)
    pick = jnp.where(onehot, 1.0, 0.0).astype(BF16)
    for g in range(NSA_KV_HEADS):
        mask_o[g] = jnp.dot(pick, masks[g].astype(BF16), preferred_element_type=F32)

    win = win_ref[...]
    kw_t, vw_t = win[:LANES].astype(BF16), win[LANES:].astype(BF16)
    kpos = (past - w_buf) + lax.broadcasted_iota(jnp.int32, (1, w_buf), 1)
    dist_w = past - kpos
    valid_w = (dist_w >= 0) & (dist_w < NSA_WINDOW) & (kpos >= 0)
    s_w = jnp.where(valid_w, _nn(q2b, kw_t) - slope * dist_w.astype(F32), NEG)
    new = kvw_ref[...]
    kn, vn = new[:, :LANES], new[:, LANES:]
    s_n = jnp.sum(q2 * kn, axis=-1, keepdims=True)
    m = jnp.maximum(jnp.max(s_w, axis=-1, keepdims=True), s_n)
    p_w = jnp.where(valid_w, jnp.exp(s_w - m), 0.0)
    p_n = jnp.exp(s_n - m)
    o_w = (_nt(p_w.astype(BF16), vw_t) + p_n * vn) / (jnp.sum(p_w, axis=-1, keepdims=True) + p_n)

    loc_o[...] = _gate_rows(gate, 0, 1) * o_c + _gate_rows(gate, 2, 1) * o_w


def _nsa_local_sample(q, gate, kvcmp, win_state, kvw_new, layer, past, n_list):
    DB = q.shape[0]
    nc_all = kvcmp.shape[1]
    w_buf = win_state.shape[3]
    n_pages = past // PAGE_SIZE
    return pl.pallas_call(
        functools.partial(_nsa_local_sample_kernel, past=past, w_buf=w_buf, n_pages=n_pages, n_list=n_list),
        grid=(DB,),
        in_specs=[
            pl.BlockSpec((None, 1, NSA_WIDTH), lambda b: (b, 0, 0)),
            pl.BlockSpec((None, 1, GATE_PAD), lambda b: (b, 0, 0)),
            pl.BlockSpec((None, nc_all, KV_W), lambda b: (b, 0, 0)),
            pl.BlockSpec((None, None, KV_W, w_buf), lambda b: (layer, b, 0, 0)),
            pl.BlockSpec((None, 1, KV_W), lambda b: (b, 0, 0)),
        ],
        out_specs=[pl.BlockSpec((None, NSA_HEADS, LANES), lambda b: (b, 0, 0)),
                   pl.BlockSpec((None, NSA_KV_HEADS, n_list, PAGE_SIZE), lambda b: (b, 0, 0, 0)),
                   pl.BlockSpec((None, n_list, 1), lambda b: (b, 0, 0))],
        out_shape=[jax.ShapeDtypeStruct((DB, NSA_HEADS, LANES), F32),
                   jax.ShapeDtypeStruct((DB, NSA_KV_HEADS, n_list, PAGE_SIZE), F32),
                   jax.ShapeDtypeStruct((DB, n_list, 1), jnp.int32)],
        compiler_params=_cparams(("parallel",)),
    )(q, gate, kvcmp, win_state, kvw_new)


def _nsa_sel_sample_kernel(*refs, ppc, past):
    list_ref = refs[1]
    pages = refs[2:2 + ppc]
    q_ref, gate_ref, new_ref, mask_ref, loc_ref, o_ref, q2_s, m_s, l_s, acc_s = refs[2 + ppc:]
    b, c = pl.program_id(0), pl.program_id(1)
    slope = _slope_col(_nsa_slope, NSA_HEADS, 1)

    @pl.when(c == 0)
    def _init():
        q2_s[...] = _rows1(q_ref[...], ATTN_SCALE)
        m_s[...] = jnp.full(m_s.shape, NEG, F32)
        l_s[...] = jnp.zeros(l_s.shape, F32)
        acc_s[...] = jnp.zeros(acc_s.shape, F32)

    q2 = q2_s[...]
    q2b = q2.astype(BF16)
    tiles = [r[...] for r in pages]
    ks_t = jnp.concatenate([t[:LANES] for t in tiles], axis=1).astype(BF16)
    vs_t = jnp.concatenate([t[LANES:] for t in tiles], axis=1).astype(BF16)
    n = ppc * PAGE_SIZE
    tok = lax.broadcasted_iota(jnp.int32, (1, PAGE_SIZE), 1)
    kpos = jnp.concatenate([list_ref[b, c * ppc + i] * PAGE_SIZE + tok for i in range(ppc)], axis=1)
    dist = (past - kpos).astype(F32)
    mask = mask_ref[...]
    vrow = [jnp.concatenate([mask[g, i:i + 1, :] for i in range(ppc)], axis=1) for g in range(NSA_KV_HEADS)]
    second_group = lax.broadcasted_iota(jnp.int32, (NSA_HEADS, n), 0) >= NSA_GROUP
    valid = jnp.where(second_group, vrow[1], vrow[0]) > 0.5
    s = jnp.where(valid, _nn(q2b, ks_t) - slope * dist, NEG)
    m_old = m_s[...]
    m_new = jnp.maximum(m_old, jnp.max(s, axis=-1, keepdims=True))
    alpha = jnp.exp(m_old - m_new)
    p = jnp.where(valid, jnp.exp(s - m_new), 0.0)
    l_s[...] = alpha * l_s[...] + jnp.sum(p, axis=-1, keepdims=True)
    acc_s[...] = alpha * acc_s[...] + _nt(p.astype(BF16), vs_t)
    m_s[...] = m_new

    @pl.when(c == pl.num_programs(1) - 1)
    def _fin():
        new = new_ref[...]
        kn, vn = new[:, :LANES], new[:, LANES:]
        s_n = jnp.sum(q2 * kn, axis=-1, keepdims=True)
        m_old = m_s[...]
        m_new = jnp.maximum(m_old, s_n)
        alpha = jnp.exp(m_old - m_new)
        p_n = jnp.exp(s_n - m_new)
        o_s = (alpha * acc_s[...] + p_n * vn) / (alpha * l_s[...] + p_n)
        o_ref[...] = _from_rows(loc_ref[...] + _gate_rows(gate_ref[...], 1, 1) * o_s, 1)


def _nsa_sel_sample(pool, page_table, page_list, q, gate, kvs_new, mask, loc, layer, past, ppc):
    DB, n_list = page_list.shape

    def page_spec(i):
        return pl.BlockSpec((None, None, KV_W, PAGE_SIZE),
                            lambda b, c, pt, lst: (pt[b, lst[b, c * ppc + i]], layer, 0, 0))

    grid_spec = pltpu.PrefetchScalarGridSpec(
        num_scalar_prefetch=2,
        grid=(DB, n_list // ppc),
        in_specs=[page_spec(i) for i in range(ppc)] + [
            pl.BlockSpec((None, 1, NSA_WIDTH), lambda b, c, pt, lst: (b, 0, 0)),
            pl.BlockSpec((None, 1, GATE_PAD), lambda b, c, pt, lst: (b, 0, 0)),
            pl.BlockSpec((None, 1, KV_W), lambda b, c, pt, lst: (b, 0, 0)),
            pl.BlockSpec((None, NSA_KV_HEADS, ppc, PAGE_SIZE), lambda b, c, pt, lst: (b, 0, c, 0)),
            pl.BlockSpec((None, NSA_HEADS, LANES), lambda b, c, pt, lst: (b, 0, 0)),
        ],
        out_specs=pl.BlockSpec((None, 1, NSA_WIDTH), lambda b, c, pt, lst: (b, 0, 0)),
        scratch_shapes=[pltpu.VMEM((NSA_HEADS, LANES), F32), pltpu.VMEM((NSA_HEADS, 1), F32),
                        pltpu.VMEM((NSA_HEADS, 1), F32), pltpu.VMEM((NSA_HEADS, LANES), F32)],
    )
    return pl.pallas_call(
        functools.partial(_nsa_sel_sample_kernel, ppc=ppc, past=past),
        grid_spec=grid_spec,
        out_shape=jax.ShapeDtypeStruct((DB, 1, NSA_WIDTH), F32),
        compiler_params=_cparams(("parallel", "arbitrary")),
    )(page_table, page_list, *([pool] * ppc), q, gate, kvs_new, mask, loc)


def _diff_sample_kernel(*refs, ppc, past, lam_init):
    pages = refs[1:1 + ppc]
    q_ref, new_ref, lam_ref, subln_ref, o_ref, q2_s, m_s, l_s, acc_s = refs[1 + ppc:]
    c = pl.program_id(1)
    rows = 2 * DIFF_HEADS
    slots = 2 * DIFF_HEADS
    slope = _slope_col(_diff_slope, DIFF_HEADS, 2)

    @pl.when(c == 0)
    def _init():
        q = q_ref[...] * ATTN_SCALE
        lo_half = lax.broadcasted_iota(jnp.int32, (1, LANES), 1) < HEAD_DIM
        pieces = []
        for r in range(rows):
            piece = q[:, (r // 2) * LANES:(r // 2 + 1) * LANES]
            pieces.append(jnp.where(lo_half == (r % 2 == 0), piece, 0.0))
        q2_s[...] = jnp.concatenate(pieces, axis=0)
        m_s[...] = jnp.full(m_s.shape, NEG, F32)
        l_s[...] = jnp.zeros(l_s.shape, F32)
        acc_s[...] = jnp.zeros(acc_s.shape, F32)

    q2 = q2_s[...]
    kv = jnp.concatenate([r[...] for r in pages], axis=0).astype(BF16)
    n = ppc * PAGE_SIZE * slots
    lane = lax.broadcasted_iota(jnp.int32, (rows, n), 1)
    head = lax.broadcasted_iota(jnp.int32, (rows, n), 0) // 2
    valid = (lane % slots) == head
    kpos = c * (ppc * PAGE_SIZE) + lane[:1] // slots
    s = jnp.where(valid, _nt(q2.astype(BF16), kv) - slope * (past - kpos).astype(F32), NEG)
    m_old = m_s[...]
    m_new = jnp.maximum(m_old, jnp.max(s, axis=-1, keepdims=True))
    alpha = jnp.exp(m_old - m_new)
    p = jnp.exp(s - m_new)
    l_s[...] = alpha * l_s[...] + jnp.sum(p, axis=-1, keepdims=True)
    acc_s[...] = alpha * acc_s[...] + _nn(pltpu.roll(p, DIFF_HEADS, 1).astype(BF16), kv)
    m_s[...] = m_new

    @pl.when(c == pl.num_programs(1) - 1)
    def _fin():
        new = new_ref[...]
        kn = jnp.concatenate([new[:, (r // 2) * LANES:(r // 2 + 1) * LANES] for r in range(rows)], axis=0)
        vn = jnp.concatenate([new[:, DIFF_WIDTH + (r // 2) * LANES:DIFF_WIDTH + (r // 2 + 1) * LANES]
                              for r in range(rows)], axis=0)
        s_n = jnp.sum(q2 * kn, axis=-1, keepdims=True)
        m_old = m_s[...]
        m_new = jnp.maximum(m_old, s_n)
        alpha = jnp.exp(m_old - m_new)
        p_n = jnp.exp(s_n - m_new)
        o = (alpha * acc_s[...] + p_n * vn) / (alpha * l_s[...] + p_n)
        lam = _lambda_value(lam_ref[...], lam_init)
        outs = [_subln(o[2 * h:2 * h + 1] - lam * o[2 * h + 1:2 * h + 2], subln_ref[...], lam_init)
                for h in range(DIFF_HEADS)]
        o_ref[...] = jnp.concatenate(outs, axis=1)


def _diff_sample(pool, page_table, qd, kvd_new, lam_p, subln, lam_init, layer, past, ppc):
    DB, n_pages = page_table.shape
    grid_spec = pltpu.PrefetchScalarGridSpec(
        num_scalar_prefetch=1,
        grid=(DB, n_pages // ppc),
        in_specs=_page_specs(ppc, layer, PAGE_SIZE * 2 * DIFF_HEADS, DIFF_V_DIM) + [
            pl.BlockSpec((None, 1, DIFF_WIDTH), lambda b, c, pt: (b, 0, 0)),
            pl.BlockSpec((None, 1, 2 * DIFF_WIDTH), lambda b, c, pt: (b, 0, 0)),
            pl.BlockSpec(lam_p.shape, lambda b, c, pt: (0, 0)),
            pl.BlockSpec(subln.shape, lambda b, c, pt: (0, 0)),
        ],
        out_specs=pl.BlockSpec((None, 1, DIFF_WIDTH), lambda b, c, pt: (b, 0, 0)),
        scratch_shapes=[pltpu.VMEM((2 * DIFF_HEADS, DIFF_V_DIM), F32), pltpu.VMEM((2 * DIFF_HEADS, 1), F32),
                        pltpu.VMEM((2 * DIFF_HEADS, 1), F32), pltpu.VMEM((2 * DIFF_HEADS, DIFF_V_DIM), F32)],
    )
    return pl.pallas_call(
        functools.partial(_diff_sample_kernel, ppc=ppc, past=past, lam_init=lam_init),
        grid_spec=grid_spec,
        out_shape=jax.ShapeDtypeStruct((DB, 1, DIFF_WIDTH), F32),
        compiler_params=_cparams(("parallel", "arbitrary")),
    )(page_table, *([pool] * ppc), qd, kvd_new, lam_p, subln)


def _pick(n, prefs):
    for p in prefs:
        if n % p == 0:
            return p
    return n


def _ffn_chunk(dff):
    for k in (2, 1, 4, 11, 22):
        if dff % k == 0 and (dff // k) % LANES == 0 and dff // k <= 1536:
            return dff // k
    return dff


def kernel(x_prompt, x_sample, cache_nsa_cmp, cache_nsa_slc, cache_diff, state_nsa_win, state_ffn_conv,
           page_table, norm_attn, w_in, qk_gain_nsa, qk_gain_diff, nsa_cmp_w, diff_lambda, diff_subln,
           w_br_a, w_br_b, w_o, norm_ffn, w_up, conv_w, conv_b, w_down):
    B, S, D = x_prompt.shape
    DB, t_new, _ = x_sample.shape
    assert t_new == 1, "the sample group carries one new token per sequence"
    depth = w_in.shape[0]
    n_pages = page_table.shape[1]
    past = n_pages * PAGE_SIZE
    n_phys = cache_nsa_cmp.shape[0]
    w_buf = state_nsa_win.shape[2]
    dff = w_down.shape[1]
    assert S % PAGE_SIZE == 0 and S >= NSA_WINDOW + PAGE_SIZE

    tm_proj = _pick(S, (256, 128))
    tm_row = _pick(S, (512, 256, 128))
    tq_nsa = 128
    tk_nsa = _pick(S, (512, 256, 128))
    tq_diff = _pick(S, (512, 256, 128))
    tk_diff = tq_diff
    fc = _ffn_chunk(dff)
    ppc = _pick(n_pages, (8, 4, 2, 1))
    ppc_cmp = _pick(n_pages, (8, 4))
    ppc_sel = 8
    n_sel_s = min(NSA_TOPK, -(-(past + 1) // NSA_BLOCK))
    n_list = min(n_pages, -(-NSA_KV_HEADS * n_sel_s // ppc_sel) * ppc_sel)
    assert n_list % ppc_sel == 0

    chan_major = lambda a: jnp.transpose(a, (0, 1, 3, 4, 5, 2)).reshape(a.shape[0], a.shape[1], KV_W, a.shape[2])
    pool_cmp = chan_major(cache_nsa_cmp)
    pool_slc = chan_major(cache_nsa_slc)
    pool_diff = cache_diff.reshape(n_phys, depth, PAGE_SIZE * 2 * DIFF_HEADS, DIFF_V_DIM)
    win_state = chan_major(state_nsa_win)
    pmat = _avg_matrix()
    kaux = _key_aux(S)

    xp = x_prompt.reshape(B * S, D)
    xs = x_sample.reshape(DB, D)
    outs_p = {k: [] for k in ("cmp", "slc", "diff", "win", "conv")}
    outs_s = {k: [] for k in ("cmp", "slc", "diff", "win", "conv")}
    w_keep_p = min(NSA_WINDOW, S)
    w_keep_s = min(NSA_WINDOW, w_buf + 1)

    for l in range(depth):
        lam_init = 0.8 - 0.6 * math.exp(-0.3 * l)
        w_packed = _pack_w_in(w_in[l])
        gains = _pack_gains(qk_gain_nsa[l], qk_gain_diff[l])
        gn_a = norm_attn[l][None, :]
        gn_f = norm_ffn[l][None, :]
        wexp = _cmp_weights(nsa_cmp_w[l])
        w8 = _cmp_page_weights(nsa_cmp_w[l])
        wa, wb, wo = w_br_a[l].astype(BF16), w_br_b[l].astype(BF16), w_o[l].astype(BF16)
        wup, wdn = w_up[l].astype(BF16), w_down[l].astype(BF16)
        lam_p = diff_lambda[l].astype(F32)
        subln = diff_subln[l][None, :]

        q, kvc, kvs, kvw, gate, qd, kvd, gm = _proj(xp, gn_a, w_packed, gains, pmat, tm_proj)
        r3 = lambda a: a.reshape(B, S, a.shape[-1])
        kvcmp = _compress_prompt(r3(kvc), wexp, tk_nsa)
        o_a = _nsa_prompt(r3(q), r3(gate), kvcmp, r3(kvw), r3(kvs), kaux, tq_nsa, tk_nsa)
        o_b = _diff_prompt(r3(qd), r3(kvd), kaux, lam_p, subln, lam_init, tq_diff, tk_diff)
        hp = _merge(xp, o_a.reshape(B * S, -1), o_b.reshape(B * S, -1), gm, wa, wb, wo, tm_row)
        xp, conv_p = _ffn(hp, gn_f, wup, conv_w[l], conv_b[l], wdn, tm_row, fc, seq_len=S)
        outs_p["cmp"].append(kvc.reshape(B, S, 2, NSA_KV_HEADS, HEAD_DIM))
        outs_p["slc"].append(kvs.reshape(B, S, 2, NSA_KV_HEADS, HEAD_DIM))
        outs_p["diff"].append(kvd.reshape(B, S, 2, DIFF_HEADS, DIFF_V_DIM))
        outs_p["win"].append(r3(kvw)[:, S - w_keep_p:].reshape(B, w_keep_p, 2, NSA_KV_HEADS, HEAD_DIM))
        outs_p["conv"].append(conv_p)

        q, kvc, kvs, kvw, gate, qd, kvd, gm = _proj(xs, gn_a, w_packed, gains, pmat, DB)
        r1 = lambda a: a.reshape(DB, 1, a.shape[-1])
        kvcmp = _compress_pages(pool_cmp, page_table, w8, l, ppc_cmp)
        loc, mask, plist = _nsa_local_sample(r1(q), r1(gate), kvcmp, win_state, r1(kvw), l, past, n_list)
        o_a = _nsa_sel_sample(pool_slc, page_table, plist.reshape(DB, n_list), r1(q), r1(gate), r1(kvs), mask, loc,
                              l, past, ppc_sel)
        o_b = _diff_sample(pool_diff, page_table, r1(qd), r1(kvd), lam_p, subln, lam_init, l, past, ppc)
        hs = _merge(xs, o_a.reshape(DB, -1), o_b.reshape(DB, -1), gm, wa, wb, wo, DB)
        xs, conv_s = _ffn(hs, gn_f, wup, conv_w[l], conv_b[l], wdn, DB, fc, state=state_ffn_conv[l])
        outs_s["cmp"].append(kvc.reshape(DB, 1, 2, NSA_KV_HEADS, HEAD_DIM))
        outs_s["slc"].append(kvs.reshape(DB, 1, 2, NSA_KV_HEADS, HEAD_DIM))
        outs_s["diff"].append(kvd.reshape(DB, 1, 2, DIFF_HEADS, DIFF_V_DIM))
        win_all = jnp.concatenate([win_state[l], kvw[:, :, None]], axis=2)[:, :, w_buf + 1 - w_keep_s:]
        outs_s["win"].append(jnp.transpose(win_all.reshape(DB, 2, NSA_KV_HEADS, HEAD_DIM, w_keep_s), (0, 4, 1, 2, 3)))
        outs_s["conv"].append(conv_s)

    return (xp.reshape(B, S, D), xs.reshape(DB, 1, D),
            jnp.stack(outs_p["cmp"], axis=1), jnp.stack(outs_p["slc"], axis=1), jnp.stack(outs_p["diff"], axis=1),
            jnp.stack(outs_p["win"], axis=0), jnp.stack(outs_p["conv"], axis=0),
            jnp.stack(outs_s["cmp"], axis=1), jnp.stack(outs_s["slc"], axis=1), jnp.stack(outs_s["diff"], axis=1),
            jnp.stack(outs_s["win"], axis=0), jnp.stack(outs_s["conv"], axis=0))
```

```python
import functools
import math

import numpy as np
import jax
import jax.numpy as jnp
from jax import lax
from jax.experimental import pallas as pl
from jax.experimental.pallas import tpu as pltpu

HEAD_DIM = 64
NSA_HEADS = 8
NSA_KV_HEADS = 2
NSA_GROUP = NSA_HEADS // NSA_KV_HEADS
NSA_BLOCK = 64
NSA_TOPK = 16
NSA_WINDOW = 512
NSA_WIDTH = NSA_HEADS * HEAD_DIM
DIFF_HEADS = 4
DIFF_V_DIM = 2 * HEAD_DIM
DIFF_WIDTH = DIFF_HEADS * DIFF_V_DIM
CONV_W = 3
PAGE_SIZE = 128
RMS_EPS = 1e-6
ATTN_SCALE = HEAD_DIM ** -0.5
FORCE_SCORE = 1e4
KV_W = 2 * NSA_KV_HEADS * HEAD_DIM
LANES = 128
GATE_PAD = LANES
NEG = -1e30
VMEM_LIMIT = 56 * 1024 * 1024

F32 = jnp.float32
BF16 = jnp.bfloat16
HIGHEST = lax.Precision.HIGHEST

C_Q = 0
C_KVC = C_Q + NSA_WIDTH
C_KVS = C_KVC + KV_W
C_KVW = C_KVS + KV_W
C_GATE = C_KVW + KV_W
C_QD = C_GATE + GATE_PAD
C_KD = C_QD + DIFF_WIDTH
C_VD = C_KD + DIFF_WIDTH
C_GM = C_VD + DIFF_WIDTH


def _nsa_slope(h):
    g, r = h // NSA_GROUP, h % NSA_GROUP
    return 2.0 ** (-8.0 * (r * NSA_KV_HEADS + g + 1) / NSA_HEADS)


def _diff_slope(h):
    return 2.0 ** (-8.0 * (h + 1) / DIFF_HEADS)


LOG2E = math.log2(math.e)
MASK_BIG = 2.0 ** 100
AUX_BIAS = HEAD_DIM


def _split3(x):
    parts, rest = [], np.float64(x)
    for _ in range(3):
        p = np.float64(np.asarray(rest, dtype=np.float32).astype(jnp.bfloat16).astype(np.float32))
        parts.append(float(p))
        rest = rest - p
    return parts


def _bias_q_rows(slopes):
    rows = np.zeros((len(slopes), LANES), np.float32)
    for i, s in enumerate(slopes):
        rows[i, AUX_BIAS:AUX_BIAS + 6] = _split3(s * LOG2E) * 2
    return jnp.asarray(rows)


def _key_aux(n_keys):
    kpos = np.arange(n_keys)
    aux = np.zeros((n_keys, LANES), np.float32)
    blk = kpos // NSA_BLOCK
    ok = blk < AUX_BIAS
    aux[kpos[ok], blk[ok]] = 1.0
    aux[:, AUX_BIAS:AUX_BIAS + 3] = (NSA_BLOCK * blk)[:, None]
    aux[:, AUX_BIAS + 3:AUX_BIAS + 6] = (kpos % NSA_BLOCK)[:, None]
    return jnp.asarray(aux, dtype=BF16)


FLASH_ROWS = 1024


def _flash_update(qc_s, kc, v, m_s, l_s, acc_s, causal=None, groups=1):
    rows = qc_s.shape[0]
    qc, m_all, l_all, acc_all = qc_s[...], m_s[...], l_s[...], acc_s[...]
    m_out, l_out, acc_out = [], [], []
    chunk = min(FLASH_ROWS, rows)
    for r0 in range(0, rows, chunk):
        rs = slice(r0, r0 + chunk)
        s = _nt(qc[rs, :], kc)
        if causal is not None:
            t, n = causal.shape
            if chunk >= t:
                s = jnp.where(causal[None], s.reshape(chunk // t, t, n), NEG).reshape(chunk, n)
            else:
                s = jnp.where(causal[r0 % t:r0 % t + chunk], s, NEG)
        m_old = m_all[rs, :]
        m_new = jnp.maximum(m_old, jnp.max(s, axis=-1, keepdims=True))
        alpha = jnp.exp2(m_old - m_new)
        p = jnp.exp2(s - jnp.concatenate([m_new] * (s.shape[1] // LANES), axis=1))
        l_out.append(alpha * l_all[rs, :] + jnp.sum(p, axis=-1, keepdims=True))
        acc_out.append(alpha * acc_all[rs, :] + _nn(p.astype(BF16), v))
        m_out.append(m_new)
    m_s[...] = jnp.concatenate(m_out, axis=0)
    l_s[...] = jnp.concatenate(l_out, axis=0)
    acc_s[...] = jnp.concatenate(acc_out, axis=0)


def _nt(a, b, **kw):
    return lax.dot_general(a, b, (((1,), (1,)), ((), ())), preferred_element_type=F32, **kw)


def _nn(a, b, **kw):
    return lax.dot_general(a, b, (((1,), (0,)), ((), ())), preferred_element_type=F32, **kw)


def _cparams(sem):
    return pltpu.CompilerParams(dimension_semantics=sem, vmem_limit_bytes=VMEM_LIMIT)


def _proj_kernel(x_ref, gn_ref, w_ref, gains_ref, pmat_ref,
                 q_o, kvc_o, kvs_o, kvw_o, gate_o, qd_o, kvd_o, gm_o, *cache_o):
    x = x_ref[...]
    ms = jnp.mean(x * x, axis=-1, keepdims=True)
    xn = (x * lax.rsqrt(ms + RMS_EPS) * gn_ref[...]).astype(BF16)
    pmat = pmat_ref[...]

    def seg(c0, width):
        return jnp.dot(xn, w_ref[:, c0:c0 + width], preferred_element_type=F32)

    def headnorm(z, gain_row):
        msq = jnp.dot((z * z).astype(BF16), pmat, preferred_element_type=F32)
        return z * lax.rsqrt(msq + RMS_EPS) * gains_ref[gain_row:gain_row + 1, :]

    def normed(c0, width, gain_row):
        z = seg(c0, width)
        return jnp.concatenate([headnorm(z[:, c:c + LANES], gain_row) for c in range(0, width, LANES)], axis=1)

    q_o[...] = normed(C_Q, NSA_WIDTH, 0)
    for c0, out, row in ((C_KVC, kvc_o, 1), (C_KVS, kvs_o, 2), (C_KVW, kvw_o, 3)):
        z = seg(c0, KV_W)
        out[...] = jnp.concatenate([headnorm(z[:, :LANES], row), z[:, LANES:]], axis=1)
    gate_o[...] = jax.nn.sigmoid(seg(C_GATE, GATE_PAD))
    qd_o[...] = normed(C_QD, DIFF_WIDTH, 4)
    kd, vd = normed(C_KD, DIFF_WIDTH, 5), seg(C_VD, DIFF_WIDTH)
    kvd_o[:, :DIFF_WIDTH] = kd
    kvd_o[:, DIFF_WIDTH:] = vd
    if cache_o:
        tm = x.shape[0]
        slots = 2 * DIFF_HEADS
        for j in range(DIFF_HEADS):
            cache_o[0][pl.ds(j, tm, stride=slots), :] = kd[:, j * DIFF_V_DIM:(j + 1) * DIFF_V_DIM]
            cache_o[0][pl.ds(DIFF_HEADS + j, tm, stride=slots), :] = vd[:, j * DIFF_V_DIM:(j + 1) * DIFF_V_DIM]
    d_model = x.shape[-1]
    for c in range(0, 2 * d_model, 512):
        gm_o[:, c:c + 512] = jax.nn.sigmoid(seg(C_GM + c, 512)).astype(gm_o.dtype)


def _proj(x2d, gn, w_packed, gains, pmat, tm, cache_order_copy=False):
    T, D = x2d.shape
    widths = (NSA_WIDTH, KV_W, KV_W, KV_W, GATE_PAD, DIFF_WIDTH, 2 * DIFF_WIDTH, 2 * D)
    full = lambda a: pl.BlockSpec(a.shape, lambda i: (0, 0))
    out_specs = [pl.BlockSpec((tm, w), lambda i: (i, 0)) for w in widths]
    out_shape = [jax.ShapeDtypeStruct((T, w), F32) for w in widths[:-1]]
    out_shape.append(jax.ShapeDtypeStruct((T, widths[-1]), BF16))
    if cache_order_copy:
        slots = 2 * DIFF_HEADS
        out_specs.append(pl.BlockSpec((tm * slots, DIFF_V_DIM), lambda i: (i, 0)))
        out_shape.append(jax.ShapeDtypeStruct((T * slots, DIFF_V_DIM), F32))
    return pl.pallas_call(
        _proj_kernel,
        grid=(T // tm,),
        in_specs=[pl.BlockSpec((tm, D), lambda i: (i, 0)), full(gn), full(w_packed), full(gains), full(pmat)],
        out_specs=out_specs,
        out_shape=out_shape,
        compiler_params=_cparams(("parallel",)),
    )(x2d, gn, w_packed, gains, pmat)


def _pack_w_in(w_in_l):
    D = w_in_l.shape[0]
    sizes = (NSA_WIDTH, KV_W, KV_W, KV_W, 3 * NSA_HEADS, DIFF_WIDTH, DIFF_WIDTH, DIFF_WIDTH, 2 * D)
    offs = np.cumsum(sizes)[:-1].tolist()
    q, kvc, kvs, kvw, gn, qd, kd, vd, gm = jnp.split(w_in_l, offs, axis=1)
    gn = jnp.pad(gn, ((0, 0), (0, GATE_PAD - gn.shape[1])))
    perm = lambda w: w.reshape(D, 2, DIFF_HEADS, HEAD_DIM).transpose(0, 2, 1, 3).reshape(D, DIFF_WIDTH)
    return jnp.concatenate([q, kvc, kvs, kvw, gn, perm(qd), perm(kd), vd, gm], axis=1).astype(BF16)


def _pack_gains(g_nsa, g_diff):
    two = lambda a, b: jnp.concatenate([a, b])[None, :]
    rows = [two(g_nsa[0], g_nsa[0]), two(g_nsa[1], g_nsa[1]), two(g_nsa[2], g_nsa[2]), two(g_nsa[3], g_nsa[3]),
            two(g_diff[0], g_diff[1]), two(g_diff[2], g_diff[3])]
    rows += [jnp.ones((1, LANES), F32)] * 2
    return jnp.concatenate(rows, axis=0).astype(F32)


def _avg_matrix():
    i = np.arange(LANES)
    return jnp.asarray((i[:, None] // HEAD_DIM == i[None, :] // HEAD_DIM) / HEAD_DIM, dtype=BF16)


def _compress_kernel(*refs, n_in, n_prefetch):
    refs = refs[n_prefetch:]
    w_ref, out_ref = refs[n_in], refs[n_in + 1]
    w = w_ref[...]
    outs = []
    for r in refs[:n_in]:
        x = r[...].reshape(-1, NSA_BLOCK, KV_W)
        outs.append(jnp.sum(x * w[None], axis=1))
    out_ref[...] = jnp.concatenate(outs, axis=0).reshape(out_ref.shape)


def _cmp_weights(cmp_w_l):
    return jnp.concatenate([jnp.broadcast_to(cmp_w_l[0][:, None], (NSA_BLOCK, KV_W // 2)),
                            jnp.broadcast_to(cmp_w_l[1][:, None], (NSA_BLOCK, KV_W // 2))], axis=1).astype(F32)


def _compress_prompt(kvc, wexp, tk):
    B, S, _ = kvc.shape
    nb = tk // NSA_BLOCK
    return pl.pallas_call(
        functools.partial(_compress_kernel, n_in=1, n_prefetch=0),
        grid=(B, S // tk),
        in_specs=[pl.BlockSpec((None, tk, KV_W), lambda b, i: (b, i, 0)),
                  pl.BlockSpec(wexp.shape, lambda b, i: (0, 0))],
        out_specs=pl.BlockSpec((None, nb, KV_W), lambda b, i: (b, i, 0)),
        out_shape=jax.ShapeDtypeStruct((B, S // NSA_BLOCK, KV_W), F32),
        compiler_params=_cparams(("parallel", "parallel")),
    )(kvc, wexp)


def _page_specs(n, layer, rows, cols):
    def mk(i):
        return pl.BlockSpec((None, None, rows, cols), lambda b, c, pt: (pt[b, c * n + i], layer, 0, 0))
    return [mk(i) for i in range(n)]


def _split_bf16(x):
    hi = x.astype(BF16)
    return hi, (x - hi.astype(F32)).astype(BF16)


def _compress_pages_kernel(*refs, ppc):
    pages, w_ref, out_ref = refs[1:1 + ppc], refs[1 + ppc], refs[2 + ppc]
    w_hi, w_lo = _split_bf16(w_ref[...])
    w_both = jnp.concatenate([w_hi, w_lo], axis=0)
    k_cols = lax.broadcasted_iota(jnp.int32, (2, KV_W), 1) < KV_W // 2
    outs = []
    for r in pages:
        hi, lo = _split_bf16(r[...])
        o16 = _nt(w_both, hi)
        o8 = o16[:8] + o16[8:] + _nt(w_hi, lo)
        outs.append(jnp.where(k_cols, o8[0:2], o8[2:4]))
    out_ref[...] = jnp.concatenate(outs, axis=0)


def _cmp_page_weights(cmp_w_l):
    z = jnp.zeros((NSA_BLOCK,), F32)
    rows = [jnp.concatenate([cmp_w_l[0], z]), jnp.concatenate([z, cmp_w_l[0]]),
            jnp.concatenate([cmp_w_l[1], z]), jnp.concatenate([z, cmp_w_l[1]])]
    return jnp.concatenate([jnp.stack(rows), jnp.zeros((4, PAGE_SIZE), F32)], axis=0).astype(F32)


def _compress_pages(pool_t, page_table, w8, layer, ppc):
    DB, n_pages = page_table.shape
    bpp = PAGE_SIZE // NSA_BLOCK
    grid_spec = pltpu.PrefetchScalarGridSpec(
        num_scalar_prefetch=1,
        grid=(DB, n_pages // ppc),
        in_specs=_page_specs(ppc, layer, KV_W, PAGE_SIZE) + [pl.BlockSpec(w8.shape, lambda b, c, pt: (0, 0))],
        out_specs=pl.BlockSpec((None, ppc * bpp, KV_W), lambda b, c, pt: (b, c, 0)),
    )
    return pl.pallas_call(
        functools.partial(_compress_pages_kernel, ppc=ppc),
        grid_spec=grid_spec,
        out_shape=jax.ShapeDtypeStruct((DB, n_pages * bpp, KV_W), F32),
        compiler_params=_cparams(("parallel", "parallel")),
    )(page_table, *([pool_t] * ppc), w8)


def _to_rows(q, scale):
    t = q.shape[0]
    hi_half = lax.broadcasted_iota(jnp.int32, (t, LANES), 1) >= HEAD_DIM
    pieces = []
    for h in range(NSA_HEADS):
        g = h // NSA_GROUP
        c = q[:, (h // 2) * LANES:(h // 2 + 1) * LANES]
        if (h % 2) != g:
            c = pltpu.roll(c, HEAD_DIM, 1)
        keep = hi_half if g == 1 else jnp.logical_not(hi_half)
        pieces.append(jnp.where(keep, c * scale, 0.0))
    return jnp.concatenate(pieces, axis=0)


def _from_rows(o, t):
    lo_half = lax.broadcasted_iota(jnp.int32, (t, LANES), 1) < HEAD_DIM
    chunks = []
    for j in range(NSA_HEADS // 2):
        a = o[(2 * j) * t:(2 * j + 1) * t]
        b = o[(2 * j + 1) * t:(2 * j + 2) * t]
        if j // 2 == 0:
            chunks.append(jnp.where(lo_half, a, pltpu.roll(b, HEAD_DIM, 1)))
        else:
            chunks.append(jnp.where(lo_half, pltpu.roll(a, HEAD_DIM, 1), b))
    return jnp.concatenate(chunks, axis=1)


def _gate_rows(gate, branch, t):
    cols = [jnp.broadcast_to(gate[:, branch * NSA_HEADS + h:branch * NSA_HEADS + h + 1], (t, LANES))
            for h in range(NSA_HEADS)]
    return jnp.concatenate(cols, axis=0)


def _softmax_rows(s, valid):
    m = jnp.max(jnp.where(valid, s, NEG), axis=-1, keepdims=True)
    p = jnp.where(valid, jnp.exp(s - m), 0.0)
    return p / jnp.maximum(jnp.sum(p, axis=-1, keepdims=True), 1e-30)


def _nsa_prompt_kernel(qi_tab, ki_tab, last_tab, q_ref, gate_ref, cmp_ref, kvw_ref, kaux_all_ref, bq_ref,
                       kvs_ref, kaux_ref, o_ref, qc_s, loc_s, g1_s, m_s, l_s, acc_s, *, tq, tk, seq):
    step = pl.program_id(1)
    qi, ki = qi_tab[step], ki_tab[step]
    q0 = qi * tq
    nc = seq // NSA_BLOCK
    ns = -(-seq // NSA_BLOCK)
    n_sel = min(NSA_TOPK, ns)
    wk = NSA_WINDOW + tq

    @pl.when(ki == 0)
    def _init():
        q2 = _to_rows(q_ref[...], ATTN_SCALE)
        gate = gate_ref[...]
        qpos_col = q0 + lax.broadcasted_iota(jnp.int32, (tq, 1), 0)
        qpos_row = q0 + lax.broadcasted_iota(jnp.int32, (1, tq), 1)

        cmp = cmp_ref[...]
        kc, vc = cmp[:, :LANES], cmp[:, LANES:]
        bend_row = lax.broadcasted_iota(jnp.int32, (1, nc), 1) * NSA_BLOCK + (NSA_BLOCK - 1)
        dist_c = qpos_col - bend_row
        valid_c = dist_c >= 0
        s_c = _nt(q2.astype(BF16), kc.astype(BF16))
        vcb = vc.astype(BF16)
        o_c = []
        for h in range(NSA_HEADS):
            sh = s_c[h * tq:(h + 1) * tq] - _nsa_slope(h) * dist_c.astype(F32)
            o_c.append(_nn(_softmax_rows(sh, valid_c).astype(BF16), vcb))
        o_c = jnp.concatenate(o_c, axis=0)

        bend_col = lax.broadcasted_iota(jnp.int32, (nc, 1), 0) * NSA_BLOCK + (NSA_BLOCK - 1)
        dist_t = qpos_row - bend_col
        valid_t = dist_t >= 0
        s_t = _nt(kc, q2, precision=HIGHEST)
        blk = lax.broadcasted_iota(jnp.int32, (ns, tq), 0)
        cur = qpos_row // NSA_BLOCK
        sel_t = []
        for g in range(NSA_KV_HEADS):
            imp = jnp.zeros((nc, tq), F32)
            for r in range(NSA_GROUP):
                h = g * NSA_GROUP + r
                sh = s_t[:, h * tq:(h + 1) * tq] - _nsa_slope(h) * dist_t.astype(F32)
                m = jnp.max(jnp.where(valid_t, sh, NEG), axis=0, keepdims=True)
                p = jnp.where(valid_t, jnp.exp(sh - m), 0.0)
                imp = imp + p / jnp.maximum(jnp.sum(p, axis=0, keepdims=True), 1e-30)
            if ns > nc:
                imp = jnp.concatenate([imp, jnp.zeros((ns - nc, tq), F32)], axis=0)
            forced = (blk == 0) | (blk == cur) | (blk == cur - 1)
            score = jnp.where(forced, FORCE_SCORE, jnp.where(blk > cur, -1.0, imp))
            grp = [score[j:j + 8] for j in range(0, ns, 8)]
            below = lax.broadcasted_iota(jnp.int32, (8, tq), 0)
            cnt = [jnp.zeros((8, tq), jnp.int32) for _ in grp]
            for i in range(ns):
                row = jnp.broadcast_to(score[i:i + 1, :], (8, tq))
                for j, sc in enumerate(grp):
                    if 8 * j > i:
                        ahead = row >= sc
                    elif 8 * j + 7 <= i:
                        ahead = row > sc
                    else:
                        ahead = (row > sc) | ((row == sc) & (below > i - 8 * j))
                    cnt[j] = cnt[j] + ahead.astype(jnp.int32)
            sel_t.append((jnp.concatenate(cnt, axis=0) < n_sel).astype(F32))
        sel = jnp.concatenate(sel_t, axis=0).T

        q2l = (q2 * LOG2E).astype(BF16)
        low = lax.broadcasted_iota(jnp.int32, (tq, LANES), 1) < AUX_BIAS
        pad = jnp.ones((tq, LANES - ns), F32)
        aux_w, aux_s = [], []
        for h in range(NSA_HEADS):
            g = h // NSA_GROUP
            bq = jnp.broadcast_to(bq_ref[h:h + 1, :], (tq, LANES))
            drop = (jnp.concatenate([sel[:, g * ns:(g + 1) * ns], pad], axis=1) - 1.0) * MASK_BIG
            aux_w.append(jnp.where(low, 0.0, bq))
            aux_s.append(jnp.where(low, drop, bq))
        qc_w = jnp.concatenate([q2l, jnp.concatenate(aux_w, axis=0).astype(BF16)], axis=1)
        qc_s[...] = jnp.concatenate([q2l, jnp.concatenate(aux_s, axis=0).astype(BF16)], axis=1)

        start = pl.multiple_of(jnp.clip(q0 - NSA_WINDOW, 0, seq - wk), LANES)
        kvw = kvw_ref[pl.ds(start, wk), :]
        kcw = jnp.concatenate([kvw[:, :LANES].astype(BF16), kaux_all_ref[pl.ds(start, wk), :]], axis=1)
        vw = kvw[:, LANES:].astype(BF16)
        kpos = start + lax.broadcasted_iota(jnp.int32, (1, wk), 1)
        dist_w = qpos_col - kpos
        valid_w = ((dist_w >= 0) & (dist_w < NSA_WINDOW))[None]
        s_w = jnp.where(valid_w, _nt(qc_w, kcw).reshape(NSA_HEADS, tq, wk), NEG).reshape(NSA_HEADS * tq, wk)
        p_w = jnp.exp2(s_w - jnp.max(s_w, axis=-1, keepdims=True))
        o_w = _nn(p_w.astype(BF16), vw) / jnp.sum(p_w, axis=-1, keepdims=True)

        loc_s[...] = _gate_rows(gate, 0, tq) * o_c + _gate_rows(gate, 2, tq) * o_w
        g1_s[...] = _gate_rows(gate, 1, tq)
        m_s[...] = jnp.full(m_s.shape, NEG, F32)
        l_s[...] = jnp.zeros(l_s.shape, F32)
        acc_s[...] = jnp.zeros(acc_s.shape, F32)

    kvs = kvs_ref[...]
    kc_t = jnp.concatenate([kvs[:, :LANES].astype(BF16), kaux_ref[...]], axis=1)
    vs = kvs[:, LANES:].astype(BF16)
    is_last = last_tab[step] == 1

    @pl.when(jnp.logical_not(is_last))
    def _interior():
        _flash_update(qc_s, kc_t, vs, m_s, l_s, acc_s)

    @pl.when(is_last)
    def _diag():
        kpos = ki * tk + lax.broadcasted_iota(jnp.int32, (1, tk), 1)
        qpos = q0 + lax.broadcasted_iota(jnp.int32, (tq, 1), 0)
        _flash_update(qc_s, kc_t, vs, m_s, l_s, acc_s, causal=kpos <= qpos)
        o_ref[...] = _from_rows(loc_s[...] + g1_s[...] * (acc_s[...] / l_s[...]), tq).astype(o_ref.dtype)


def _causal_pairs(seq, tq, tk):
    assert tk % tq == 0, "only the last key tile of a query tile may cross the diagonal"
    qi, ki, last = [], [], []
    for i in range(seq // tq):
        n = (i * tq + tq - 1) // tk + 1
        for j in range(n):
            qi.append(i), ki.append(j), last.append(1 if j == n - 1 else 0)
    mk = lambda a: jnp.asarray(np.asarray(a, dtype=np.int32))
    return mk(qi), mk(ki), mk(last)


def _nsa_prompt(q, gate, kvcmp, kvw, kvs, kaux, tq, tk):
    B, S, _ = q.shape
    nc = S // NSA_BLOCK
    ns = -(-S // NSA_BLOCK)
    assert ns <= AUX_BIAS, "the block one-hot must fit below the ALiBi lanes"
    qi, ki, last = _causal_pairs(S, tq, tk)
    bq = _bias_q_rows([_nsa_slope(h) for h in range(NSA_HEADS)])
    rows = NSA_HEADS * tq
    grid_spec = pltpu.PrefetchScalarGridSpec(
        num_scalar_prefetch=3,
        grid=(B, int(qi.shape[0])),
        in_specs=[
            pl.BlockSpec((None, tq, NSA_WIDTH), lambda b, s, qt, kt, lt: (b, qt[s], 0)),
            pl.BlockSpec((None, tq, GATE_PAD), lambda b, s, qt, kt, lt: (b, qt[s], 0)),
            pl.BlockSpec((None, nc, KV_W), lambda b, s, qt, kt, lt: (b, 0, 0)),
            pl.BlockSpec((None, S, KV_W), lambda b, s, qt, kt, lt: (b, 0, 0)),
            pl.BlockSpec((S, LANES), lambda b, s, qt, kt, lt: (0, 0)),
            pl.BlockSpec(bq.shape, lambda b, s, qt, kt, lt: (0, 0)),
            pl.BlockSpec((None, tk, KV_W), lambda b, s, qt, kt, lt: (b, kt[s], 0)),
            pl.BlockSpec((tk, LANES), lambda b, s, qt, kt, lt: (kt[s], 0)),
        ],
        out_specs=pl.BlockSpec((None, tq, NSA_WIDTH), lambda b, s, qt, kt, lt: (b, qt[s], 0)),
        scratch_shapes=[
            pltpu.VMEM((rows, 2 * LANES), BF16),
            pltpu.VMEM((rows, LANES), F32),
            pltpu.VMEM((rows, LANES), F32),
            pltpu.VMEM((rows, LANES), F32),
            pltpu.VMEM((rows, LANES), F32),
            pltpu.VMEM((rows, LANES), F32),
        ],
    )
    return pl.pallas_call(
        functools.partial(_nsa_prompt_kernel, tq=tq, tk=tk, seq=S),
        grid_spec=grid_spec,
        out_shape=jax.ShapeDtypeStruct((B, S, NSA_WIDTH), BF16),
        compiler_params=_cparams(("parallel", "arbitrary")),
    )(qi, ki, last, q, gate, kvcmp, kvw, kaux, bq, kvs, kaux)


def _lambda_value(lp, lam_init):
    a = jnp.sum(lp[0:1] * lp[1:2], axis=-1, keepdims=True)
    b = jnp.sum(lp[2:3] * lp[3:4], axis=-1, keepdims=True)
    return jnp.exp(a) - jnp.exp(b) + lam_init


def _subln(o, g, lam_init):
    ms = jnp.mean(o * o, axis=-1, keepdims=True)
    return o * lax.rsqrt(ms + RMS_EPS) * g * (1.0 - lam_init)


def _diff_prompt_kernel(qi_tab, ki_tab, last_tab, q_ref, k_ref, v_ref, kaux_ref, bq_ref, lam_ref, subln_ref, o_ref,
                        qc_s, m_s, l_s, acc_s, *, tq, tk, lam_init):
    step = pl.program_id(2)
    qi, ki = qi_tab[step], ki_tab[step]

    @pl.when(ki == 0)
    def _init():
        q = q_ref[...] * (ATTN_SCALE * LOG2E)
        lo_half = lax.broadcasted_iota(jnp.int32, (tq, LANES), 1) < HEAD_DIM
        q2 = jnp.concatenate([jnp.where(lo_half, q, 0.0), jnp.where(lo_half, 0.0, q)], axis=0)
        aux = jnp.broadcast_to(bq_ref[0:1, :], (2 * tq, LANES))
        qc_s[...] = jnp.concatenate([q2, aux], axis=1).astype(BF16)
        m_s[...] = jnp.full(m_s.shape, NEG, F32)
        l_s[...] = jnp.zeros(l_s.shape, F32)
        acc_s[...] = jnp.zeros(acc_s.shape, F32)

    kc = jnp.concatenate([k_ref[...].astype(BF16), kaux_ref[...]], axis=1)
    v = v_ref[...].astype(BF16)
    is_last = last_tab[step] == 1

    @pl.when(jnp.logical_not(is_last))
    def _interior():
        _flash_update(qc_s, kc, v, m_s, l_s, acc_s)

    @pl.when(is_last)
    def _diag():
        kpos = ki * tk + lax.broadcasted_iota(jnp.int32, (1, tk), 1)
        qpos = qi * tq + lax.broadcasted_iota(jnp.int32, (tq, 1), 0)
        _flash_update(qc_s, kc, v, m_s, l_s, acc_s, causal=kpos <= qpos)
        o = acc_s[...] / l_s[...]
        lam = _lambda_value(lam_ref[...], lam_init)
        o_ref[...] = _subln(o[:tq] - lam * o[tq:], subln_ref[...], lam_init).astype(o_ref.dtype)


def _diff_prompt(qd, kvd, kaux, lam_p, subln, lam_init, tq, tk):
    B, S, _ = qd.shape
    qi, ki, last = _causal_pairs(S, tq, tk)
    bq = jnp.repeat(_bias_q_rows([_diff_slope(h) for h in range(DIFF_HEADS)])[:, None, :], 8, axis=1)
    grid_spec = pltpu.PrefetchScalarGridSpec(
        num_scalar_prefetch=3,
        grid=(B, DIFF_HEADS, int(qi.shape[0])),
        in_specs=[
            pl.BlockSpec((None, tq, DIFF_V_DIM), lambda b, h, s, qt, kt, lt: (b, qt[s], h)),
            pl.BlockSpec((None, tk, DIFF_V_DIM), lambda b, h, s, qt, kt, lt: (b, kt[s], h)),
            pl.BlockSpec((None, tk, DIFF_V_DIM), lambda b, h, s, qt, kt, lt: (b, kt[s], DIFF_HEADS + h)),
            pl.BlockSpec((tk, LANES), lambda b, h, s, qt, kt, lt: (kt[s], 0)),
            pl.BlockSpec((None, 8, LANES), lambda b, h, s, qt, kt, lt: (h, 0, 0)),
            pl.BlockSpec(lam_p.shape, lambda b, h, s, qt, kt, lt: (0, 0)),
            pl.BlockSpec(subln.shape, lambda b, h, s, qt, kt, lt: (0, 0)),
        ],
        out_specs=pl.BlockSpec((None, tq, DIFF_V_DIM), lambda b, h, s, qt, kt, lt: (b, qt[s], h)),
        scratch_shapes=[
            pltpu.VMEM((2 * tq, 2 * LANES), BF16),
            pltpu.VMEM((2 * tq, LANES), F32),
            pltpu.VMEM((2 * tq, LANES), F32),
            pltpu.VMEM((2 * tq, DIFF_V_DIM), F32),
        ],
    )
    return pl.pallas_call(
        functools.partial(_diff_prompt_kernel, tq=tq, tk=tk, lam_init=lam_init),
        grid_spec=grid_spec,
        out_shape=jax.ShapeDtypeStruct((B, S, DIFF_WIDTH), BF16),
        compiler_params=_cparams(("parallel", "parallel", "arbitrary")),
    )(qi, ki, last, qd, kvd, kvd, kaux, bq, lam_p, subln)


def _merge_kernel(x_ref, oa_ref, ob_ref, gm_ref, wa_ref, wb_ref, wo_ref, h_ref):
    d = x_ref.shape[-1]
    ya = jnp.dot(oa_ref[...].astype(BF16), wa_ref[...], preferred_element_type=F32)
    yb = jnp.dot(ob_ref[...].astype(BF16), wb_ref[...], preferred_element_type=F32)
    y = gm_ref[:, :d] * ya + gm_ref[:, d:] * yb
    h_ref[...] = x_ref[...] + jnp.dot(y.astype(BF16), wo_ref[...], preferred_element_type=F32)


def _merge(x2d, oa, ob, gm, wa, wb, wo, tm):
    T, D = x2d.shape
    row = lambda w: pl.BlockSpec((tm, w), lambda i: (i, 0))
    full = lambda a: pl.BlockSpec(a.shape, lambda i: (0, 0))
    return pl.pallas_call(
        _merge_kernel,
        grid=(T // tm,),
        in_specs=[row(D), row(NSA_WIDTH), row(DIFF_WIDTH), row(2 * D), full(wa), full(wb), full(wo)],
        out_specs=row(D),
        out_shape=jax.ShapeDtypeStruct((T, D), F32),
        compiler_params=_cparams(("parallel",)),
    )(x2d, oa, ob, gm, wa, wb, wo)


def _ffn_kernel(*refs, tiles_per_seq, per_row_state):
    if per_row_state:
        (h_ref, gn_ref, wa_ref, wg_ref, cwa_ref, cwg_ref, cba_ref, cbg_ref, wd_ref, pa_ref, pg_ref,
         y_ref, ua_o, ug_o, xn_s, acc_s) = refs
    else:
        (h_ref, gn_ref, wa_ref, wg_ref, cwa_ref, cwg_ref, cba_ref, cbg_ref, wd_ref,
         y_ref, ua_o, ug_o, xn_s, acc_s, carry_s) = refs
    i, j = pl.program_id(0), pl.program_id(1)
    tm = h_ref.shape[0]

    @pl.when(j == 0)
    def _init():
        x = h_ref[...]
        ms = jnp.mean(x * x, axis=-1, keepdims=True)
        xn_s[...] = (x * lax.rsqrt(ms + RMS_EPS) * gn_ref[...]).astype(BF16)
        acc_s[...] = jnp.zeros(acc_s.shape, F32)

    if not per_row_state:
        @pl.when((i % tiles_per_seq) == 0)
        def _zero_state():
            carry_s[j] = jnp.zeros(carry_s.shape[1:], F32)

    xn = xn_s[...]
    row = lax.broadcasted_iota(jnp.int32, (8, 1), 0)

    def conv(w_ref, cw_ref, cb_ref, prev_ref, part, u_out):
        u = jnp.dot(xn, w_ref[...], preferred_element_type=F32)
        cw = cw_ref[...]
        if per_row_state:
            prev = prev_ref[...]
            p2, p1 = prev[:, 0, :], prev[:, 1, :]
            u_out[:, 0, :] = p1
            u_out[:, 1, :] = u
        else:
            carry = carry_s[j, part]
            r1, r2 = pltpu.roll(u, 1, 0), pltpu.roll(u, 2, 0)
            h1 = jnp.where(row == 0, carry[1:2, :], r1[:8])
            h2 = jnp.where(row == 0, carry[0:1, :], jnp.where(row == 1, carry[1:2, :], r2[:8]))
            p1 = jnp.concatenate([h1, r1[8:]], axis=0)
            p2 = jnp.concatenate([h2, r2[8:]], axis=0)
            tail = u[tm - 2:, :]
            carry_s[j, part] = tail
            u_out[...] = tail
        return cb_ref[...] + cw[2:3, :] * u + cw[1:2, :] * p1 + cw[0:1, :] * p2

    a = conv(wa_ref, cwa_ref, cba_ref, None if not per_row_state else pa_ref, 0, ua_o)
    g = conv(wg_ref, cwg_ref, cbg_ref, None if not per_row_state else pg_ref, 1, ug_o)
    act = (g * jax.nn.sigmoid(g) * a).astype(BF16)
    acc_s[...] += jnp.dot(act, wd_ref[...], preferred_element_type=F32)

    @pl.when(j == pl.num_programs(1) - 1)
    def _fin():
        y_ref[...] = h_ref[...] + acc_s[...]


def _ffn(h2d, gn, w_up, conv_w, conv_b, w_down, tm, fc, seq_len=None, state=None):
    T, D = h2d.shape
    dff = w_down.shape[0]
    nf = dff // fc
    per_row = state is not None
    cb2 = conv_b[None, :]
    in_specs = [
        pl.BlockSpec((tm, D), lambda i, j: (i, 0)),
        pl.BlockSpec(gn.shape, lambda i, j: (0, 0)),
        pl.BlockSpec((D, fc), lambda i, j: (0, j)),
        pl.BlockSpec((D, fc), lambda i, j: (0, nf + j)),
        pl.BlockSpec((CONV_W, fc), lambda i, j: (0, j)),
        pl.BlockSpec((CONV_W, fc), lambda i, j: (0, nf + j)),
        pl.BlockSpec((1, fc), lambda i, j: (0, j)),
        pl.BlockSpec((1, fc), lambda i, j: (0, nf + j)),
        pl.BlockSpec((fc, D), lambda i, j: (j, 0)),
    ]
    args = [h2d, gn, w_up, w_up, conv_w, conv_w, cb2, cb2, w_down]
    scratch = [pltpu.VMEM((tm, D), BF16), pltpu.VMEM((tm, D), F32)]
    if per_row:
        in_specs += [pl.BlockSpec((tm, CONV_W - 1, fc), lambda i, j: (i, 0, j)),
                     pl.BlockSpec((tm, CONV_W - 1, fc), lambda i, j: (i, 0, nf + j))]
        args += [state, state]
        n_state, tiles_per_seq = T, 1
        st_spec = pl.BlockSpec((tm, CONV_W - 1, fc), lambda i, j: (i, 0, j))
    else:
        tiles_per_seq = seq_len // tm
        n_state = T // tm
        scratch.append(pltpu.VMEM((nf, 2, CONV_W - 1, fc), F32))
        st_spec = pl.BlockSpec((None, CONV_W - 1, fc), lambda i, j: (i, 0, j))
    y, ua, ug = pl.pallas_call(
        functools.partial(_ffn_kernel, tiles_per_seq=tiles_per_seq, per_row_state=per_row),
        grid=(T // tm, nf),
        in_specs=in_specs,
        out_specs=[pl.BlockSpec((tm, D), lambda i, j: (i, 0)), st_spec, st_spec],
        out_shape=[jax.ShapeDtypeStruct((T, D), F32),
                   jax.ShapeDtypeStruct((n_state, CONV_W - 1, dff), F32),
                   jax.ShapeDtypeStruct((n_state, CONV_W - 1, dff), F32)],
        scratch_shapes=scratch,
        compiler_params=_cparams(("arbitrary", "arbitrary")),
    )(*args)
    state_out = jnp.concatenate([ua, ug], axis=-1)
    if not per_row:
        state_out = state_out[tiles_per_seq - 1::tiles_per_seq]
    return y, state_out


def _rows1(q_row, scale):
    return _to_rows(q_row, scale)


def _slope_col(fn, n, rep):
    return jnp.concatenate([jnp.full((1, 1), fn(h // rep), F32) for h in range(n * rep)], axis=0)


def _nsa_local_sample_kernel(q_ref, gate_ref, cmp_ref, win_ref, kvw_ref, loc_o, mask_o, list_o,
                             *, past, w_buf, n_pages, n_list):
    nc = (past + 1) // NSA_BLOCK
    ns = -(-(past + 1) // NSA_BLOCK)
    n_sel = min(NSA_TOPK, ns)
    nsp = -(-ns // LANES) * LANES
    bpp = PAGE_SIZE // NSA_BLOCK
    q2 = _rows1(q_ref[...], ATTN_SCALE)
    q2b = q2.astype(BF16)
    gate = gate_ref[...]
    slope = _slope_col(_nsa_slope, NSA_HEADS, 1)

    cmp = cmp_ref[...][:nc]
    kc, vc = cmp[:, :LANES], cmp[:, LANES:]
    bend = lax.broadcasted_iota(jnp.int32, (1, nc), 1) * NSA_BLOCK + (NSA_BLOCK - 1)
    dist_c = past - bend
    valid_c = dist_c >= 0
    p_c = _softmax_rows(_nt(q2, kc, precision=HIGHEST) - slope * dist_c.astype(F32), valid_c)
    o_c = _nn(p_c, vc, precision=HIGHEST)

    lane = lax.broadcasted_iota(jnp.int32, (1, nsp), 1)
    cur = past // NSA_BLOCK
    forced = (lane == 0) | (lane == cur) | (lane == cur - 1)
    ii = lax.broadcasted_iota(jnp.int32, (nsp, nsp), 0)
    jj = lax.broadcasted_iota(jnp.int32, (nsp, nsp), 1)
    pg = lax.broadcasted_iota(jnp.int32, (n_pages, nsp), 0)
    nn = lax.broadcasted_iota(jnp.int32, (n_pages, nsp), 1)
    tok_lo = lax.broadcasted_iota(jnp.int32, (n_pages, PAGE_SIZE), 1) < NSA_BLOCK
    masks, need = [], jnp.zeros((n_pages, 1), F32)
    for g in range(NSA_KV_HEADS):
        imp = jnp.sum(p_c[g * NSA_GROUP:(g + 1) * NSA_GROUP], axis=0, keepdims=True)
        imp = jnp.concatenate([imp, jnp.zeros((1, nsp - nc), F32)], axis=1)
        score = jnp.where(forced, FORCE_SCORE, jnp.where(lane > cur, -1.0, imp))
        score = jnp.where(lane < ns, score, -3e38)
        col = jnp.sum(jnp.where(ii == jj, score, 0.0), axis=1, keepdims=True)
        ahead = (col > score) | ((col == score) & (ii < jj))
        cnt = jnp.sum(ahead.astype(jnp.int32), axis=0, keepdims=True)
        sel = ((cnt < n_sel) & (lane < ns)).astype(F32)
        c0 = jnp.sum(jnp.where(nn == pg * bpp, sel, 0.0), axis=1, keepdims=True)
        c1 = jnp.sum(jnp.where(nn == pg * bpp + 1, sel, 0.0), axis=1, keepdims=True)
        masks.append(jnp.where(tok_lo, c0, c1))
        need = need + c0 + c1

    pi = lax.broadcasted_iota(jnp.int32, (n_pages, n_pages), 0)
    pj = lax.broadcasted_iota(jnp.int32, (n_pages, n_pages), 1)
    needed = (need > 0.5).astype(F32)
    need_row = jnp.sum(jnp.where(pi == pj, needed, 0.0), axis=0, keepdims=True)
    before = jnp.sum(jnp.where(pi < pj, needed, 0.0), axis=0, keepdims=True)
    slot = lax.broadcasted_iota(jnp.int32, (n_list, n_pages), 0).astype(F32)
    onehot = (slot == before) & (need_row > 0.5)
    page = lax.broadcasted_iota(jnp.int32, (n_list, n_pages), 1).astype(F32)
    list_o[...] = jnp.sum(jnp.where(onehot, page, 0.0), axis=1, keepdims=True).astype(jnp.int32)
    pick = jnp.where(onehot, 1.0, 0.0).astype(BF16)
    for g in range(NSA_KV_HEADS):
        mask_o[g] = jnp.dot(pick, masks[g].astype(BF16), preferred_element_type=F32)

    win = win_ref[...]
    kw_t, vw_t = win[:LANES].astype(BF16), win[LANES:].astype(BF16)
    kpos = (past - w_buf) + lax.broadcasted_iota(jnp.int32, (1, w_buf), 1)
    dist_w = past - kpos
    valid_w = (dist_w >= 0) & (dist_w < NSA_WINDOW) & (kpos >= 0)
    s_w = jnp.where(valid_w, _nn(q2b, kw_t) - slope * dist_w.astype(F32), NEG)
    new = kvw_ref[...]
    kn, vn = new[:, :LANES], new[:, LANES:]
    s_n = jnp.sum(q2 * kn, axis=-1, keepdims=True)
    m = jnp.maximum(jnp.max(s_w, axis=-1, keepdims=True), s_n)
    p_w = jnp.where(valid_w, jnp.exp(s_w - m), 0.0)
    p_n = jnp.exp(s_n - m)
    o_w = (_nt(p_w.astype(BF16), vw_t) + p_n * vn) / (jnp.sum(p_w, axis=-1, keepdims=True) + p_n)

    loc_o[...] = _gate_rows(gate, 0, 1) * o_c + _gate_rows(gate, 2, 1) * o_w


def _nsa_local_sample(q, gate, kvcmp, win_state, kvw_new, layer, past, n_list):
    DB = q.shape[0]
    nc_all = kvcmp.shape[1]
    w_buf = win_state.shape[3]
    n_pages = past // PAGE_SIZE
    return pl.pallas_call(
        functools.partial(_nsa_local_sample_kernel, past=past, w_buf=w_buf, n_pages=n_pages, n_list=n_list),
        grid=(DB,),
        in_specs=[
            pl.BlockSpec((None, 1, NSA_WIDTH), lambda b: (b, 0, 0)),
            pl.BlockSpec((None, 1, GATE_PAD), lambda b: (b, 0, 0)),
            pl.BlockSpec((None, nc_all, KV_W), lambda b: (b, 0, 0)),
            pl.BlockSpec((None, None, KV_W, w_buf), lambda b: (layer, b, 0, 0)),
            pl.BlockSpec((None, 1, KV_W), lambda b: (b, 0, 0)),
        ],
        out_specs=[pl.BlockSpec((None, NSA_HEADS, LANES), lambda b: (b, 0, 0)),
                   pl.BlockSpec((None, NSA_KV_HEADS, n_list, PAGE_SIZE), lambda b: (b, 0, 0, 0)),
                   pl.BlockSpec((None, n_list, 1), lambda b: (b, 0, 0))],
        out_shape=[jax.ShapeDtypeStruct((DB, NSA_HEADS, LANES), F32),
                   jax.ShapeDtypeStruct((DB, NSA_KV_HEADS, n_list, PAGE_SIZE), F32),
                   jax.ShapeDtypeStruct((DB, n_list, 1), jnp.int32)],
        compiler_params=_cparams(("parallel",)),
    )(q, gate, kvcmp, win_state, kvw_new)


def _nsa_sel_sample_kernel(*refs, ppc, past):
    list_ref = refs[1]
    pages = refs[2:2 + ppc]
    q_ref, gate_ref, new_ref, mask_ref, loc_ref, o_ref, q2_s, m_s, l_s, acc_s = refs[2 + ppc:]
    b, c = pl.program_id(0), pl.program_id(1)
    slope = _slope_col(_nsa_slope, NSA_HEADS, 1)

    @pl.when(c == 0)
    def _init():
        q2_s[...] = _rows1(q_ref[...], ATTN_SCALE)
        m_s[...] = jnp.full(m_s.shape, NEG, F32)
        l_s[...] = jnp.zeros(l_s.shape, F32)
        acc_s[...] = jnp.zeros(acc_s.shape, F32)

    q2 = q2_s[...]
    q2b = q2.astype(BF16)
    tiles = [r[...] for r in pages]
    ks_t = jnp.concatenate([t[:LANES] for t in tiles], axis=1).astype(BF16)
    vs_t = jnp.concatenate([t[LANES:] for t in tiles], axis=1).astype(BF16)
    n = ppc * PAGE_SIZE
    tok = lax.broadcasted_iota(jnp.int32, (1, PAGE_SIZE), 1)
    kpos = jnp.concatenate([list_ref[b, c * ppc + i] * PAGE_SIZE + tok for i in range(ppc)], axis=1)
    dist = (past - kpos).astype(F32)
    mask = mask_ref[...]
    vrow = [jnp.concatenate([mask[g, i:i + 1, :] for i in range(ppc)], axis=1) for g in range(NSA_KV_HEADS)]
    second_group = lax.broadcasted_iota(jnp.int32, (NSA_HEADS, n), 0) >= NSA_GROUP
    valid = jnp.where(second_group, vrow[1], vrow[0]) > 0.5
    s = jnp.where(valid, _nn(q2b, ks_t) - slope * dist, NEG)
    m_old = m_s[...]
    m_new = jnp.maximum(m_old, jnp.max(s, axis=-1, keepdims=True))
    alpha = jnp.exp(m_old - m_new)
    p = jnp.where(valid, jnp.exp(s - m_new), 0.0)
    l_s[...] = alpha * l_s[...] + jnp.sum(p, axis=-1, keepdims=True)
    acc_s[...] = alpha * acc_s[...] + _nt(p.astype(BF16), vs_t)
    m_s[...] = m_new

    @pl.when(c == pl.num_programs(1) - 1)
    def _fin():
        new = new_ref[...]
        kn, vn = new[:, :LANES], new[:, LANES:]
        s_n = jnp.sum(q2 * kn, axis=-1, keepdims=True)
        m_old = m_s[...]
        m_new = jnp.maximum(m_old, s_n)
        alpha = jnp.exp(m_old - m_new)
        p_n = jnp.exp(s_n - m_new)
        o_s = (alpha * acc_s[...] + p_n * vn) / (alpha * l_s[...] + p_n)
        o_ref[...] = _from_rows(loc_ref[...] + _gate_rows(gate_ref[...], 1, 1) * o_s, 1)


def _nsa_sel_sample(pool, page_table, page_list, q, gate, kvs_new, mask, loc, layer, past, ppc):
    DB, n_list = page_list.shape

    def page_spec(i):
        return pl.BlockSpec((None, None, KV_W, PAGE_SIZE),
                            lambda b, c, pt, lst: (pt[b, lst[b, c * ppc + i]], layer, 0, 0))

    grid_spec = pltpu.PrefetchScalarGridSpec(
        num_scalar_prefetch=2,
        grid=(DB, n_list // ppc),
        in_specs=[page_spec(i) for i in range(ppc)] + [
            pl.BlockSpec((None, 1, NSA_WIDTH), lambda b, c, pt, lst: (b, 0, 0)),
            pl.BlockSpec((None, 1, GATE_PAD), lambda b, c, pt, lst: (b, 0, 0)),
            pl.BlockSpec((None, 1, KV_W), lambda b, c, pt, lst: (b, 0, 0)),
            pl.BlockSpec((None, NSA_KV_HEADS, ppc, PAGE_SIZE), lambda b, c, pt, lst: (b, 0, c, 0)),
            pl.BlockSpec((None, NSA_HEADS, LANES), lambda b, c, pt, lst: (b, 0, 0)),
        ],
        out_specs=pl.BlockSpec((None, 1, NSA_WIDTH), lambda b, c, pt, lst: (b, 0, 0)),
        scratch_shapes=[pltpu.VMEM((NSA_HEADS, LANES), F32), pltpu.VMEM((NSA_HEADS, 1), F32),
                        pltpu.VMEM((NSA_HEADS, 1), F32), pltpu.VMEM((NSA_HEADS, LANES), F32)],
    )
    return pl.pallas_call(
        functools.partial(_nsa_sel_sample_kernel, ppc=ppc, past=past),
        grid_spec=grid_spec,
        out_shape=jax.ShapeDtypeStruct((DB, 1, NSA_WIDTH), F32),
        compiler_params=_cparams(("parallel", "arbitrary")),
    )(page_table, page_list, *([pool] * ppc), q, gate, kvs_new, mask, loc)


def _diff_sample_kernel(*refs, ppc, past, lam_init):
    pages = refs[1:1 + ppc]
    q_ref, new_ref, lam_ref, subln_ref, o_ref, q2_s, m_s, l_s, acc_s = refs[1 + ppc:]
    c = pl.program_id(1)
    rows = 2 * DIFF_HEADS
    slots = 2 * DIFF_HEADS
    slope = _slope_col(_diff_slope, DIFF_HEADS, 2)

    @pl.when(c == 0)
    def _init():
        q = q_ref[...] * ATTN_SCALE
        lo_half = lax.broadcasted_iota(jnp.int32, (1, LANES), 1) < HEAD_DIM
        pieces = []
        for r in range(rows):
            piece = q[:, (r // 2) * LANES:(r // 2 + 1) * LANES]
            pieces.append(jnp.where(lo_half == (r % 2 == 0), piece, 0.0))
        q2_s[...] = jnp.concatenate(pieces, axis=0)
        m_s[...] = jnp.full(m_s.shape, NEG, F32)
        l_s[...] = jnp.zeros(l_s.shape, F32)
        acc_s[...] = jnp.zeros(acc_s.shape, F32)

    q2 = q2_s[...]
    kv = jnp.concatenate([r[...] for r in pages], axis=0).astype(BF16)
    n = ppc * PAGE_SIZE * slots
    lane = lax.broadcasted_iota(jnp.int32, (rows, n), 1)
    head = lax.broadcasted_iota(jnp.int32, (rows, n), 0) // 2
    valid = (lane % slots) == head
    kpos = c * (ppc * PAGE_SIZE) + lane[:1] // slots
    s = jnp.where(valid, _nt(q2.astype(BF16), kv) - slope * (past - kpos).astype(F32), NEG)
    m_old = m_s[...]
    m_new = jnp.maximum(m_old, jnp.max(s, axis=-1, keepdims=True))
    alpha = jnp.exp(m_old - m_new)
    p = jnp.exp(s - m_new)
    l_s[...] = alpha * l_s[...] + jnp.sum(p, axis=-1, keepdims=True)
    acc_s[...] = alpha * acc_s[...] + _nn(pltpu.roll(p, DIFF_HEADS, 1).astype(BF16), kv)
    m_s[...] = m_new

    @pl.when(c == pl.num_programs(1) - 1)
    def _fin():
        new = new_ref[...]
        kn = jnp.concatenate([new[:, (r // 2) * LANES:(r // 2 + 1) * LANES] for r in range(rows)], axis=0)
        vn = jnp.concatenate([new[:, DIFF_WIDTH + (r // 2) * LANES:DIFF_WIDTH + (r // 2 + 1) * LANES]
                              for r in range(rows)], axis=0)
        s_n = jnp.sum(q2 * kn, axis=-1, keepdims=True)
        m_old = m_s[...]
        m_new = jnp.maximum(m_old, s_n)
        alpha = jnp.exp(m_old - m_new)
        p_n = jnp.exp(s_n - m_new)
        o = (alpha * acc_s[...] + p_n * vn) / (alpha * l_s[...] + p_n)
        lam = _lambda_value(lam_ref[...], lam_init)
        outs = [_subln(o[2 * h:2 * h + 1] - lam * o[2 * h + 1:2 * h + 2], subln_ref[...], lam_init)
                for h in range(DIFF_HEADS)]
        o_ref[...] = jnp.concatenate(outs, axis=1)


def _diff_sample(pool, page_table, qd, kvd_new, lam_p, subln, lam_init, layer, past, ppc):
    DB, n_pages = page_table.shape
    grid_spec = pltpu.PrefetchScalarGridSpec(
        num_scalar_prefetch=1,
        grid=(DB, n_pages // ppc),
        in_specs=_page_specs(ppc, layer, PAGE_SIZE * 2 * DIFF_HEADS, DIFF_V_DIM) + [
            pl.BlockSpec((None, 1, DIFF_WIDTH), lambda b, c, pt: (b, 0, 0)),
            pl.BlockSpec((None, 1, 2 * DIFF_WIDTH), lambda b, c, pt: (b, 0, 0)),
            pl.BlockSpec(lam_p.shape, lambda b, c, pt: (0, 0)),
            pl.BlockSpec(subln.shape, lambda b, c, pt: (0, 0)),
        ],
        out_specs=pl.BlockSpec((None, 1, DIFF_WIDTH), lambda b, c, pt: (b, 0, 0)),
        scratch_shapes=[pltpu.VMEM((2 * DIFF_HEADS, DIFF_V_DIM), F32), pltpu.VMEM((2 * DIFF_HEADS, 1), F32),
                        pltpu.VMEM((2 * DIFF_HEADS, 1), F32), pltpu.VMEM((2 * DIFF_HEADS, DIFF_V_DIM), F32)],
    )
    return pl.pallas_call(
        functools.partial(_diff_sample_kernel, ppc=ppc, past=past, lam_init=lam_init),
        grid_spec=grid_spec,
        out_shape=jax.ShapeDtypeStruct((DB, 1, DIFF_WIDTH), F32),
        compiler_params=_cparams(("parallel", "arbitrary")),
    )(page_table, *([pool] * ppc), qd, kvd_new, lam_p, subln)


def _pick(n, prefs):
    for p in prefs:
        if n % p == 0:
            return p
    return n


def _ffn_chunk(dff):
    for k in (2, 1, 4, 11, 22):
        if dff % k == 0 and (dff // k) % LANES == 0 and dff // k <= 1536:
            return dff // k
    return dff


def kernel(x_prompt, x_sample, cache_nsa_cmp, cache_nsa_slc, cache_diff, state_nsa_win, state_ffn_conv,
           page_table, norm_attn, w_in, qk_gain_nsa, qk_gain_diff, nsa_cmp_w, diff_lambda, diff_subln,
           w_br_a, w_br_b, w_o, norm_ffn, w_up, conv_w, conv_b, w_down):
    B, S, D = x_prompt.shape
    DB, t_new, _ = x_sample.shape
    assert t_new == 1, "the sample group carries one new token per sequence"
    depth = w_in.shape[0]
    n_pages = page_table.shape[1]
    past = n_pages * PAGE_SIZE
    n_phys = cache_nsa_cmp.shape[0]
    w_buf = state_nsa_win.shape[2]
    dff = w_down.shape[1]
    assert S % PAGE_SIZE == 0 and S >= NSA_WINDOW + PAGE_SIZE

    tm_proj = _pick(S, (256, 128))
    tm_row = _pick(S, (512, 256, 128))
    tq_nsa = 128
    tk_nsa = _pick(S, (512, 256, 128))
    tq_diff = _pick(S, (512, 256, 128))
    tk_diff = tq_diff
    fc = _ffn_chunk(dff)
    ppc = _pick(n_pages, (16, 8, 4, 2, 1))
    ppc_cmp = _pick(n_pages, (16, 8, 4))
    ppc_sel = 8
    n_sel_s = min(NSA_TOPK, -(-(past + 1) // NSA_BLOCK))
    n_list = min(n_pages, -(-NSA_KV_HEADS * n_sel_s // ppc_sel) * ppc_sel)
    assert n_list % ppc_sel == 0

    chan_major = lambda a: jnp.transpose(a, (0, 1, 3, 4, 5, 2)).reshape(a.shape[0], a.shape[1], KV_W, a.shape[2])
    pool_cmp = chan_major(cache_nsa_cmp)
    pool_slc = chan_major(cache_nsa_slc)
    pool_diff = cache_diff.reshape(n_phys, depth, PAGE_SIZE * 2 * DIFF_HEADS, DIFF_V_DIM)
    win_state = chan_major(state_nsa_win)
    pmat = _avg_matrix()
    kaux = _key_aux(S)

    xp = x_prompt.reshape(B * S, D)
    xs = x_sample.reshape(DB, D)
    outs_p = {k: [] for k in ("cmp", "slc", "diff", "win", "conv")}
    outs_s = {k: [] for k in ("cmp", "slc", "diff", "win", "conv")}
    w_keep_p = min(NSA_WINDOW, S)
    w_keep_s = min(NSA_WINDOW, w_buf + 1)

    for l in range(depth):
        lam_init = 0.8 - 0.6 * math.exp(-0.3 * l)
        w_packed = _pack_w_in(w_in[l])
        gains = _pack_gains(qk_gain_nsa[l], qk_gain_diff[l])
        gn_a = norm_attn[l][None, :]
        gn_f = norm_ffn[l][None, :]
        wexp = _cmp_weights(nsa_cmp_w[l])
        w8 = _cmp_page_weights(nsa_cmp_w[l])
        wa, wb, wo = w_br_a[l].astype(BF16), w_br_b[l].astype(BF16), w_o[l].astype(BF16)
        wup, wdn = w_up[l].astype(BF16), w_down[l].astype(BF16)
        lam_p = diff_lambda[l].astype(F32)
        subln = diff_subln[l][None, :]

        q, kvc, kvs, kvw, gate, qd, kvd, gm, kvd_cache = _proj(xp, gn_a, w_packed, gains, pmat, tm_proj,
                                                                cache_order_copy=True)
        r3 = lambda a: a.reshape(B, S, a.shape[-1])
        kvcmp = _compress_prompt(r3(kvc), wexp, tk_nsa)
        o_a = _nsa_prompt(r3(q), r3(gate), kvcmp, r3(kvw), r3(kvs), kaux, tq_nsa, tk_nsa)
        o_b = _diff_prompt(r3(qd), r3(kvd), kaux, lam_p, subln, lam_init, tq_diff, tk_diff)
        hp = _merge(xp, o_a.reshape(B * S, -1), o_b.reshape(B * S, -1), gm, wa, wb, wo, tm_row)
        xp, conv_p = _ffn(hp, gn_f, wup, conv_w[l], conv_b[l], wdn, tm_row, fc, seq_len=S)
        outs_p["cmp"].append(kvc.reshape(B, S, 2, NSA_KV_HEADS, HEAD_DIM))
        outs_p["slc"].append(kvs.reshape(B, S, 2, NSA_KV_HEADS, HEAD_DIM))
        outs_p["diff"].append(kvd_cache.reshape(B, S, 2, DIFF_HEADS, DIFF_V_DIM))
        outs_p["win"].append(r3(kvw)[:, S - w_keep_p:].reshape(B, w_keep_p, 2, NSA_KV_HEADS, HEAD_DIM))
        outs_p["conv"].append(conv_p)

        q, kvc, kvs, kvw, gate, qd, kvd, gm = _proj(xs, gn_a, w_packed, gains, pmat, DB)
        r1 = lambda a: a.reshape(DB, 1, a.shape[-1])
        kvcmp = _compress_pages(pool_cmp, page_table, w8, l, ppc_cmp)
        loc, mask, plist = _nsa_local_sample(r1(q), r1(gate), kvcmp, win_state, r1(kvw), l, past, n_list)
        o_a = _nsa_sel_sample(pool_slc, page_table, plist.reshape(DB, n_list), r1(q), r1(gate), r1(kvs), mask, loc,
                              l, past, ppc_sel)
        o_b = _diff_sample(pool_diff, page_table, r1(qd), r1(kvd), lam_p, subln, lam_init, l, past, ppc)
        hs = _merge(xs, o_a.reshape(DB, -1), o_b.reshape(DB, -1), gm, wa, wb, wo, DB)
        xs, conv_s = _ffn(hs, gn_f, wup, conv_w[l], conv_b[l], wdn, DB, fc, state=state_ffn_conv[l])
        outs_s["cmp"].append(kvc.reshape(DB, 1, 2, NSA_KV_HEADS, HEAD_DIM))
        outs_s["slc"].append(kvs.reshape(DB, 1, 2, NSA_KV_HEADS, HEAD_DIM))
        outs_s["diff"].append(kvd.reshape(DB, 1, 2, DIFF_HEADS, DIFF_V_DIM))
        win_all = jnp.concatenate([win_state[l], kvw[:, :, None]], axis=2)[:, :, w_buf + 1 - w_keep_s:]
        outs_s["win"].append(jnp.transpose(win_all.reshape(DB, 2, NSA_KV_HEADS, HEAD_DIM, w_keep_s), (0, 4, 1, 2, 3)))
        outs_s["conv"].append(conv_s)

    return (xp.reshape(B, S, D), xs.reshape(DB, 1, D),
            jnp.stack(outs_p["cmp"], axis=1), jnp.stack(outs_p["slc"], axis=1), jnp.stack(outs_p["diff"], axis=1),
            jnp.stack(outs_p["win"], axis=0), jnp.stack(outs_p["conv"], axis=0),
            jnp.stack(outs_s["cmp"], axis=1), jnp.stack(outs_s["slc"], axis=1), jnp.stack(outs_s["diff"], axis=1),
            jnp.stack(outs_s["win"], axis=0), jnp.stack(outs_s["conv"], axis=0))
```

```python
import functools
import math

import numpy as np
import jax
import jax.numpy as jnp
from jax import lax
from jax.experimental import pallas as pl
from jax.experimental.pallas import tpu as pltpu

HEAD_DIM = 64
NSA_HEADS = 8
NSA_KV_HEADS = 2
NSA_GROUP = NSA_HEADS // NSA_KV_HEADS
NSA_BLOCK = 64
NSA_TOPK = 16
NSA_WINDOW = 512
NSA_WIDTH = NSA_HEADS * HEAD_DIM
DIFF_HEADS = 4
DIFF_V_DIM = 2 * HEAD_DIM
DIFF_WIDTH = DIFF_HEADS * DIFF_V_DIM
CONV_W = 3
PAGE_SIZE = 128
RMS_EPS = 1e-6
ATTN_SCALE = HEAD_DIM ** -0.5
FORCE_SCORE = 1e4
KV_W = 2 * NSA_KV_HEADS * HEAD_DIM
LANES = 128
GATE_PAD = LANES
NEG = -1e30
VMEM_LIMIT = 56 * 1024 * 1024

F32 = jnp.float32
BF16 = jnp.bfloat16
HIGHEST = lax.Precision.HIGHEST

C_Q = 0
C_KVC = C_Q + NSA_WIDTH
C_KVS = C_KVC + KV_W
C_KVW = C_KVS + KV_W
C_GATE = C_KVW + KV_W
C_QD = C_GATE + GATE_PAD
C_KD = C_QD + DIFF_WIDTH
C_VD = C_KD + DIFF_WIDTH
C_GM = C_VD + DIFF_WIDTH


def _nsa_slope(h):
    g, r = h // NSA_GROUP, h % NSA_GROUP
    return 2.0 ** (-8.0 * (r * NSA_KV_HEADS + g + 1) / NSA_HEADS)


def _diff_slope(h):
    return 2.0 ** (-8.0 * (h + 1) / DIFF_HEADS)


LOG2E = math.log2(math.e)
MASK_BIG = 2.0 ** 100
AUX_BIAS = HEAD_DIM


def _split3(x):
    parts, rest = [], np.float64(x)
    for _ in range(3):
        p = np.float64(np.asarray(rest, dtype=np.float32).astype(jnp.bfloat16).astype(np.float32))
        parts.append(float(p))
        rest = rest - p
    return parts


def _bias_q_rows(slopes):
    rows = np.zeros((len(slopes), LANES), np.float32)
    for i, s in enumerate(slopes):
        rows[i, AUX_BIAS:AUX_BIAS + 6] = _split3(s * LOG2E) * 2
    return jnp.asarray(rows)


def _key_aux(n_keys):
    kpos = np.arange(n_keys)
    aux = np.zeros((n_keys, LANES), np.float32)
    blk = kpos // NSA_BLOCK
    ok = blk < AUX_BIAS
    aux[kpos[ok], blk[ok]] = 1.0
    aux[:, AUX_BIAS:AUX_BIAS + 3] = (NSA_BLOCK * blk)[:, None]
    aux[:, AUX_BIAS + 3:AUX_BIAS + 6] = (kpos % NSA_BLOCK)[:, None]
    return jnp.asarray(aux, dtype=BF16)


FLASH_ROWS = 1024


def _flash_update(qc_s, kc, v, m_s, l_s, acc_s, causal=None, groups=1):
    rows = qc_s.shape[0]
    qc, m_all, l_all, acc_all = qc_s[...], m_s[...], l_s[...], acc_s[...]
    m_out, l_out, acc_out = [], [], []
    chunk = min(FLASH_ROWS, rows)
    for r0 in range(0, rows, chunk):
        rs = slice(r0, r0 + chunk)
        s = _nt(qc[rs, :], kc)
        if causal is not None:
            t, n = causal.shape
            if chunk >= t:
                s = jnp.where(causal[None], s.reshape(chunk // t, t, n), NEG).reshape(chunk, n)
            else:
                s = jnp.where(causal[r0 % t:r0 % t + chunk], s, NEG)
        m_old = m_all[rs, :]
        m_new = jnp.maximum(m_old, jnp.max(s, axis=-1, keepdims=True))
        alpha = jnp.exp2(m_old - m_new)
        p = jnp.exp2(s - jnp.concatenate([m_new] * (s.shape[1] // LANES), axis=1))
        l_out.append(alpha * l_all[rs, :] + jnp.sum(p, axis=-1, keepdims=True))
        acc_out.append(alpha * acc_all[rs, :] + _nn(p.astype(BF16), v))
        m_out.append(m_new)
    m_s[...] = jnp.concatenate(m_out, axis=0)
    l_s[...] = jnp.concatenate(l_out, axis=0)
    acc_s[...] = jnp.concatenate(acc_out, axis=0)


def _nt(a, b, **kw):
    return lax.dot_general(a, b, (((1,), (1,)), ((), ())), preferred_element_type=F32, **kw)


def _nn(a, b, **kw):
    return lax.dot_general(a, b, (((1,), (0,)), ((), ())), preferred_element_type=F32, **kw)


def _cparams(sem):
    return pltpu.CompilerParams(dimension_semantics=sem, vmem_limit_bytes=VMEM_LIMIT)


def _proj_kernel(*refs, n_carried):
    x_ref, gn_ref, w_ref, gains_ref, pmat_ref = refs[:5]
    q_o, kvc_o, kvs_o, kvw_o, gate_o, qd_o, kvd_o, gm_o = refs[5 + n_carried:13 + n_carried]
    cache_o = refs[13 + n_carried:]
    x = x_ref[...]
    ms = jnp.mean(x * x, axis=-1, keepdims=True)
    xn = (x * lax.rsqrt(ms + RMS_EPS) * gn_ref[...]).astype(BF16)
    pmat = pmat_ref[...]

    def seg(c0, width):
        return jnp.dot(xn, w_ref[:, c0:c0 + width], preferred_element_type=F32)

    def headnorm(z, gain_row):
        msq = jnp.dot((z * z).astype(BF16), pmat, preferred_element_type=F32)
        return z * lax.rsqrt(msq + RMS_EPS) * gains_ref[gain_row:gain_row + 1, :]

    def normed(c0, width, gain_row):
        z = seg(c0, width)
        return jnp.concatenate([headnorm(z[:, c:c + LANES], gain_row) for c in range(0, width, LANES)], axis=1)

    q_o[...] = normed(C_Q, NSA_WIDTH, 0)
    for c0, out, row in ((C_KVC, kvc_o, 1), (C_KVS, kvs_o, 2), (C_KVW, kvw_o, 3)):
        z = seg(c0, KV_W)
        out[...] = jnp.concatenate([headnorm(z[:, :LANES], row), z[:, LANES:]], axis=1)
    gate_o[...] = jax.nn.sigmoid(seg(C_GATE, GATE_PAD))
    qd_o[...] = normed(C_QD, DIFF_WIDTH, 4)
    kd, vd = normed(C_KD, DIFF_WIDTH, 5), seg(C_VD, DIFF_WIDTH)
    kvd_o[:, :DIFF_WIDTH] = kd
    kvd_o[:, DIFF_WIDTH:] = vd
    if cache_o:
        tm = x.shape[0]
        slots = 2 * DIFF_HEADS
        for j in range(DIFF_HEADS):
            cache_o[0][pl.ds(j, tm, stride=slots), :] = kd[:, j * DIFF_V_DIM:(j + 1) * DIFF_V_DIM]
            cache_o[0][pl.ds(DIFF_HEADS + j, tm, stride=slots), :] = vd[:, j * DIFF_V_DIM:(j + 1) * DIFF_V_DIM]
    d_model = x.shape[-1]
    for c in range(0, 2 * d_model, 512):
        gm_o[:, c:c + 512] = jax.nn.sigmoid(seg(C_GM + c, 512)).astype(gm_o.dtype)


def _proj(x2d, gn, w_packed, gains, pmat, tm, cache_out=None):
    T, D = x2d.shape
    widths = (NSA_WIDTH, KV_W, KV_W, KV_W, GATE_PAD, DIFF_WIDTH, 2 * DIFF_WIDTH, 2 * D)
    full = lambda a: pl.BlockSpec(a.shape, lambda i: (0, 0))
    in_specs = [pl.BlockSpec((tm, D), lambda i: (i, 0)), full(gn), full(w_packed), full(gains), full(pmat)]
    args = [x2d, gn, w_packed, gains, pmat]
    out_specs = [pl.BlockSpec((tm, w), lambda i: (i, 0)) for w in widths]
    out_shape = [jax.ShapeDtypeStruct((T, w), F32) for w in widths[:-1]]
    out_shape.append(jax.ShapeDtypeStruct((T, widths[-1]), BF16))
    aliases, n_carried = {}, 0
    if cache_out is not None:
        batch, seq, depth, layer, carried = cache_out
        slots = 2 * DIFF_HEADS
        tiles = seq // tm
        out_specs.append(pl.BlockSpec((None, None, tm * slots, DIFF_V_DIM),
                                      lambda i: (i // tiles, layer, i % tiles, 0)))
        out_shape.append(jax.ShapeDtypeStruct((batch, depth, seq * slots, DIFF_V_DIM), F32))
        if carried is not None:
            in_specs.append(pl.BlockSpec(memory_space=pl.ANY))
            args.append(carried)
            aliases, n_carried = {len(args) - 1: len(out_shape) - 1}, 1
    return pl.pallas_call(
        functools.partial(_proj_kernel, n_carried=n_carried),
        grid=(T // tm,),
        in_specs=in_specs,
        out_specs=out_specs,
        out_shape=out_shape,
        input_output_aliases=aliases,
        compiler_params=_cparams(("parallel",)),
    )(*args)


def _pack_w_in(w_in_l):
    D = w_in_l.shape[0]
    sizes = (NSA_WIDTH, KV_W, KV_W, KV_W, 3 * NSA_HEADS, DIFF_WIDTH, DIFF_WIDTH, DIFF_WIDTH, 2 * D)
    offs = np.cumsum(sizes)[:-1].tolist()
    q, kvc, kvs, kvw, gn, qd, kd, vd, gm = jnp.split(w_in_l, offs, axis=1)
    gn = jnp.pad(gn, ((0, 0), (0, GATE_PAD - gn.shape[1])))
    perm = lambda w: w.reshape(D, 2, DIFF_HEADS, HEAD_DIM).transpose(0, 2, 1, 3).reshape(D, DIFF_WIDTH)
    return jnp.concatenate([q, kvc, kvs, kvw, gn, perm(qd), perm(kd), vd, gm], axis=1).astype(BF16)


def _pack_gains(g_nsa, g_diff):
    two = lambda a, b: jnp.concatenate([a, b])[None, :]
    rows = [two(g_nsa[0], g_nsa[0]), two(g_nsa[1], g_nsa[1]), two(g_nsa[2], g_nsa[2]), two(g_nsa[3], g_nsa[3]),
            two(g_diff[0], g_diff[1]), two(g_diff[2], g_diff[3])]
    rows += [jnp.ones((1, LANES), F32)] * 2
    return jnp.concatenate(rows, axis=0).astype(F32)


def _avg_matrix():
    i = np.arange(LANES)
    return jnp.asarray((i[:, None] // HEAD_DIM == i[None, :] // HEAD_DIM) / HEAD_DIM, dtype=BF16)


def _compress_kernel(*refs, n_in, n_prefetch):
    refs = refs[n_prefetch:]
    w_ref, out_ref = refs[n_in], refs[n_in + 1]
    w = w_ref[...]
    outs = []
    for r in refs[:n_in]:
        x = r[...].reshape(-1, NSA_BLOCK, KV_W)
        outs.append(jnp.sum(x * w[None], axis=1))
    out_ref[...] = jnp.concatenate(outs, axis=0).reshape(out_ref.shape)


def _cmp_weights(cmp_w_l):
    return jnp.concatenate([jnp.broadcast_to(cmp_w_l[0][:, None], (NSA_BLOCK, KV_W // 2)),
                            jnp.broadcast_to(cmp_w_l[1][:, None], (NSA_BLOCK, KV_W // 2))], axis=1).astype(F32)


def _compress_prompt(kvc, wexp, tk):
    B, S, _ = kvc.shape
    nb = tk // NSA_BLOCK
    return pl.pallas_call(
        functools.partial(_compress_kernel, n_in=1, n_prefetch=0),
        grid=(B, S // tk),
        in_specs=[pl.BlockSpec((None, tk, KV_W), lambda b, i: (b, i, 0)),
                  pl.BlockSpec(wexp.shape, lambda b, i: (0, 0))],
        out_specs=pl.BlockSpec((None, nb, KV_W), lambda b, i: (b, i, 0)),
        out_shape=jax.ShapeDtypeStruct((B, S // NSA_BLOCK, KV_W), F32),
        compiler_params=_cparams(("parallel", "parallel")),
    )(kvc, wexp)


def _page_specs(n, layer, rows, cols):
    def mk(i):
        return pl.BlockSpec((None, None, rows, cols), lambda b, c, pt: (pt[b, c * n + i], layer, 0, 0))
    return [mk(i) for i in range(n)]


def _split_bf16(x):
    hi = x.astype(BF16)
    return hi, (x - hi.astype(F32)).astype(BF16)


def _compress_pages_kernel(*refs, ppc):
    pages, w_ref, out_ref = refs[1:1 + ppc], refs[1 + ppc], refs[2 + ppc]
    w_hi, w_lo = _split_bf16(w_ref[...])
    w_both = jnp.concatenate([w_hi, w_lo], axis=0)
    k_cols = lax.broadcasted_iota(jnp.int32, (2, KV_W), 1) < KV_W // 2
    outs = []
    for r in pages:
        hi, lo = _split_bf16(r[...])
        o16 = _nt(w_both, hi)
        o8 = o16[:8] + o16[8:] + _nt(w_hi, lo)
        outs.append(jnp.where(k_cols, o8[0:2], o8[2:4]))
    out_ref[...] = jnp.concatenate(outs, axis=0)


def _cmp_page_weights(cmp_w_l):
    z = jnp.zeros((NSA_BLOCK,), F32)
    rows = [jnp.concatenate([cmp_w_l[0], z]), jnp.concatenate([z, cmp_w_l[0]]),
            jnp.concatenate([cmp_w_l[1], z]), jnp.concatenate([z, cmp_w_l[1]])]
    return jnp.concatenate([jnp.stack(rows), jnp.zeros((4, PAGE_SIZE), F32)], axis=0).astype(F32)


def _compress_pages(pool_t, page_table, w8, layer, ppc):
    DB, n_pages = page_table.shape
    bpp = PAGE_SIZE // NSA_BLOCK
    grid_spec = pltpu.PrefetchScalarGridSpec(
        num_scalar_prefetch=1,
        grid=(DB, n_pages // ppc),
        in_specs=_page_specs(ppc, layer, KV_W, PAGE_SIZE) + [pl.BlockSpec(w8.shape, lambda b, c, pt: (0, 0))],
        out_specs=pl.BlockSpec((None, ppc * bpp, KV_W), lambda b, c, pt: (b, c, 0)),
    )
    return pl.pallas_call(
        functools.partial(_compress_pages_kernel, ppc=ppc),
        grid_spec=grid_spec,
        out_shape=jax.ShapeDtypeStruct((DB, n_pages * bpp, KV_W), F32),
        compiler_params=_cparams(("parallel", "parallel")),
    )(page_table, *([pool_t] * ppc), w8)


def _to_rows(q, scale):
    t = q.shape[0]
    hi_half = lax.broadcasted_iota(jnp.int32, (t, LANES), 1) >= HEAD_DIM
    pieces = []
    for h in range(NSA_HEADS):
        g = h // NSA_GROUP
        c = q[:, (h // 2) * LANES:(h // 2 + 1) * LANES]
        if (h % 2) != g:
            c = pltpu.roll(c, HEAD_DIM, 1)
        keep = hi_half if g == 1 else jnp.logical_not(hi_half)
        pieces.append(jnp.where(keep, c * scale, 0.0))
    return jnp.concatenate(pieces, axis=0)


def _from_rows(o, t):
    lo_half = lax.broadcasted_iota(jnp.int32, (t, LANES), 1) < HEAD_DIM
    chunks = []
    for j in range(NSA_HEADS // 2):
        a = o[(2 * j) * t:(2 * j + 1) * t]
        b = o[(2 * j + 1) * t:(2 * j + 2) * t]
        if j // 2 == 0:
            chunks.append(jnp.where(lo_half, a, pltpu.roll(b, HEAD_DIM, 1)))
        else:
            chunks.append(jnp.where(lo_half, pltpu.roll(a, HEAD_DIM, 1), b))
    return jnp.concatenate(chunks, axis=1)


def _gate_rows(gate, branch, t):
    cols = [jnp.broadcast_to(gate[:, branch * NSA_HEADS + h:branch * NSA_HEADS + h + 1], (t, LANES))
            for h in range(NSA_HEADS)]
    return jnp.concatenate(cols, axis=0)


def _softmax_rows(s, valid):
    m = jnp.max(jnp.where(valid, s, NEG), axis=-1, keepdims=True)
    p = jnp.where(valid, jnp.exp(s - m), 0.0)
    return p / jnp.maximum(jnp.sum(p, axis=-1, keepdims=True), 1e-30)


def _nsa_prompt_kernel(qi_tab, ki_tab, last_tab, q_ref, gate_ref, cmp_ref, kvw_ref, kaux_all_ref, bq_ref,
                       kvs_ref, kaux_ref, o_ref, qc_s, loc_s, g1_s, m_s, l_s, acc_s, *, tq, tk, seq):
    step = pl.program_id(1)
    qi, ki = qi_tab[step], ki_tab[step]
    q0 = qi * tq
    nc = seq // NSA_BLOCK
    ns = -(-seq // NSA_BLOCK)
    n_sel = min(NSA_TOPK, ns)
    wk = NSA_WINDOW + tq

    @pl.when(ki == 0)
    def _init():
        q2 = _to_rows(q_ref[...], ATTN_SCALE)
        gate = gate_ref[...]
        qpos_col = q0 + lax.broadcasted_iota(jnp.int32, (tq, 1), 0)
        qpos_row = q0 + lax.broadcasted_iota(jnp.int32, (1, tq), 1)

        cmp = cmp_ref[...]
        kc, vc = cmp[:, :LANES], cmp[:, LANES:]
        bend_row = lax.broadcasted_iota(jnp.int32, (1, nc), 1) * NSA_BLOCK + (NSA_BLOCK - 1)
        dist_c = qpos_col - bend_row
        valid_c = dist_c >= 0
        s_c = _nt(q2.astype(BF16), kc.astype(BF16))
        vcb = vc.astype(BF16)
        o_c = []
        for h in range(NSA_HEADS):
            sh = s_c[h * tq:(h + 1) * tq] - _nsa_slope(h) * dist_c.astype(F32)
            o_c.append(_nn(_softmax_rows(sh, valid_c).astype(BF16), vcb))
        o_c = jnp.concatenate(o_c, axis=0)

        bend_col = lax.broadcasted_iota(jnp.int32, (nc, 1), 0) * NSA_BLOCK + (NSA_BLOCK - 1)
        dist_t = qpos_row - bend_col
        valid_t = dist_t >= 0
        kc_hi, kc_lo = _split_bf16(kc)
        q2_hi, q2_lo = _split_bf16(q2)
        s_t = _nt(kc_hi, q2_hi) + _nt(kc_lo, q2_hi) + _nt(kc_hi, q2_lo)
        blk = lax.broadcasted_iota(jnp.int32, (ns, tq), 0)
        cur = qpos_row // NSA_BLOCK
        sel_t = []
        for g in range(NSA_KV_HEADS):
            imp = jnp.zeros((nc, tq), F32)
            for r in range(NSA_GROUP):
                h = g * NSA_GROUP + r
                sh = s_t[:, h * tq:(h + 1) * tq] - _nsa_slope(h) * dist_t.astype(F32)
                m = jnp.max(jnp.where(valid_t, sh, NEG), axis=0, keepdims=True)
                p = jnp.where(valid_t, jnp.exp(sh - m), 0.0)
                imp = imp + p / jnp.maximum(jnp.sum(p, axis=0, keepdims=True), 1e-30)
            if ns > nc:
                imp = jnp.concatenate([imp, jnp.zeros((ns - nc, tq), F32)], axis=0)
            forced = (blk == 0) | (blk == cur) | (blk == cur - 1)
            score = jnp.where(forced, FORCE_SCORE, jnp.where(blk > cur, -1.0, imp))
            grp = [score[j:j + 8] for j in range(0, ns, 8)]
            below = lax.broadcasted_iota(jnp.int32, (8, tq), 0)
            cnt = [jnp.zeros((8, tq), jnp.int32) for _ in grp]
            for i in range(ns):
                row = jnp.broadcast_to(score[i:i + 1, :], (8, tq))
                for j, sc in enumerate(grp):
                    if 8 * j > i:
                        ahead = row >= sc
                    elif 8 * j + 7 <= i:
                        ahead = row > sc
                    else:
                        ahead = (row > sc) | ((row == sc) & (below > i - 8 * j))
                    cnt[j] = cnt[j] + ahead.astype(jnp.int32)
            sel_t.append((jnp.concatenate(cnt, axis=0) < n_sel).astype(F32))
        sel = jnp.concatenate(sel_t, axis=0).T

        q2l = (q2 * LOG2E).astype(BF16)
        low = lax.broadcasted_iota(jnp.int32, (tq, LANES), 1) < AUX_BIAS
        pad = jnp.ones((tq, LANES - ns), F32)
        aux_w, aux_s = [], []
        for h in range(NSA_HEADS):
            g = h // NSA_GROUP
            bq = jnp.broadcast_to(bq_ref[h:h + 1, :], (tq, LANES))
            drop = (jnp.concatenate([sel[:, g * ns:(g + 1) * ns], pad], axis=1) - 1.0) * MASK_BIG
            aux_w.append(jnp.where(low, 0.0, bq))
            aux_s.append(jnp.where(low, drop, bq))
        qc_w = jnp.concatenate([q2l, jnp.concatenate(aux_w, axis=0).astype(BF16)], axis=1)
        qc_s[...] = jnp.concatenate([q2l, jnp.concatenate(aux_s, axis=0).astype(BF16)], axis=1)

        start = pl.multiple_of(jnp.clip(q0 - NSA_WINDOW, 0, seq - wk), LANES)
        kvw = kvw_ref[pl.ds(start, wk), :]
        kcw = jnp.concatenate([kvw[:, :LANES].astype(BF16), kaux_all_ref[pl.ds(start, wk), :]], axis=1)
        vw = kvw[:, LANES:].astype(BF16)
        kpos = start + lax.broadcasted_iota(jnp.int32, (1, wk), 1)
        dist_w = qpos_col - kpos
        valid_w = ((dist_w >= 0) & (dist_w < NSA_WINDOW))[None]
        s_w = jnp.where(valid_w, _nt(qc_w, kcw).reshape(NSA_HEADS, tq, wk), NEG).reshape(NSA_HEADS * tq, wk)
        p_w = jnp.exp2(s_w - jnp.max(s_w, axis=-1, keepdims=True))
        o_w = _nn(p_w.astype(BF16), vw) / jnp.sum(p_w, axis=-1, keepdims=True)

        loc_s[...] = _gate_rows(gate, 0, tq) * o_c + _gate_rows(gate, 2, tq) * o_w
        g1_s[...] = _gate_rows(gate, 1, tq)
        m_s[...] = jnp.full(m_s.shape, NEG, F32)
        l_s[...] = jnp.zeros(l_s.shape, F32)
        acc_s[...] = jnp.zeros(acc_s.shape, F32)

    kvs = kvs_ref[...]
    kc_t = jnp.concatenate([kvs[:, :LANES].astype(BF16), kaux_ref[...]], axis=1)
    vs = kvs[:, LANES:].astype(BF16)
    is_last = last_tab[step] == 1

    @pl.when(jnp.logical_not(is_last))
    def _interior():
        _flash_update(qc_s, kc_t, vs, m_s, l_s, acc_s)

    @pl.when(is_last)
    def _diag():
        kpos = ki * tk + lax.broadcasted_iota(jnp.int32, (1, tk), 1)
        qpos = q0 + lax.broadcasted_iota(jnp.int32, (tq, 1), 0)
        _flash_update(qc_s, kc_t, vs, m_s, l_s, acc_s, causal=kpos <= qpos)
        o_ref[...] = _from_rows(loc_s[...] + g1_s[...] * (acc_s[...] / l_s[...]), tq).astype(o_ref.dtype)


def _causal_pairs(seq, tq, tk):
    assert tk % tq == 0, "only the last key tile of a query tile may cross the diagonal"
    qi, ki, last = [], [], []
    for i in range(seq // tq):
        n = (i * tq + tq - 1) // tk + 1
        for j in range(n):
            qi.append(i), ki.append(j), last.append(1 if j == n - 1 else 0)
    mk = lambda a: jnp.asarray(np.asarray(a, dtype=np.int32))
    return mk(qi), mk(ki), mk(last)


def _nsa_prompt(q, gate, kvcmp, kvw, kvs, kaux, tq, tk):
    B, S, _ = q.shape
    nc = S // NSA_BLOCK
    ns = -(-S // NSA_BLOCK)
    assert ns <= AUX_BIAS, "the block one-hot must fit below the ALiBi lanes"
    qi, ki, last = _causal_pairs(S, tq, tk)
    bq = _bias_q_rows([_nsa_slope(h) for h in range(NSA_HEADS)])
    rows = NSA_HEADS * tq
    grid_spec = pltpu.PrefetchScalarGridSpec(
        num_scalar_prefetch=3,
        grid=(B, int(qi.shape[0])),
        in_specs=[
            pl.BlockSpec((None, tq, NSA_WIDTH), lambda b, s, qt, kt, lt: (b, qt[s], 0)),
            pl.BlockSpec((None, tq, GATE_PAD), lambda b, s, qt, kt, lt: (b, qt[s], 0)),
            pl.BlockSpec((None, nc, KV_W), lambda b, s, qt, kt, lt: (b, 0, 0)),
            pl.BlockSpec((None, S, KV_W), lambda b, s, qt, kt, lt: (b, 0, 0)),
            pl.BlockSpec((S, LANES), lambda b, s, qt, kt, lt: (0, 0)),
            pl.BlockSpec(bq.shape, lambda b, s, qt, kt, lt: (0, 0)),
            pl.BlockSpec((None, tk, KV_W), lambda b, s, qt, kt, lt: (b, kt[s], 0)),
            pl.BlockSpec((tk, LANES), lambda b, s, qt, kt, lt: (kt[s], 0)),
        ],
        out_specs=pl.BlockSpec((None, tq, NSA_WIDTH), lambda b, s, qt, kt, lt: (b, qt[s], 0)),
        scratch_shapes=[
            pltpu.VMEM((rows, 2 * LANES), BF16),
            pltpu.VMEM((rows, LANES), F32),
            pltpu.VMEM((rows, LANES), F32),
            pltpu.VMEM((rows, LANES), F32),
            pltpu.VMEM((rows, LANES), F32),
            pltpu.VMEM((rows, LANES), F32),
        ],
    )
    return pl.pallas_call(
        functools.partial(_nsa_prompt_kernel, tq=tq, tk=tk, seq=S),
        grid_spec=grid_spec,
        out_shape=jax.ShapeDtypeStruct((B, S, NSA_WIDTH), BF16),
        compiler_params=_cparams(("parallel", "arbitrary")),
    )(qi, ki, last, q, gate, kvcmp, kvw, kaux, bq, kvs, kaux)


def _lambda_value(lp, lam_init):
    a = jnp.sum(lp[0:1] * lp[1:2], axis=-1, keepdims=True)
    b = jnp.sum(lp[2:3] * lp[3:4], axis=-1, keepdims=True)
    return jnp.exp(a) - jnp.exp(b) + lam_init


def _subln(o, g, lam_init):
    ms = jnp.mean(o * o, axis=-1, keepdims=True)
    return o * lax.rsqrt(ms + RMS_EPS) * g * (1.0 - lam_init)


def _diff_prompt_kernel(qi_tab, ki_tab, last_tab, q_ref, k_ref, v_ref, kaux_ref, bq_ref, lam_ref, subln_ref, o_ref,
                        qc_s, m_s, l_s, acc_s, *, tq, tk, lam_init):
    step = pl.program_id(2)
    qi, ki = qi_tab[step], ki_tab[step]

    @pl.when(ki == 0)
    def _init():
        q = q_ref[...] * (ATTN_SCALE * LOG2E)
        lo_half = lax.broadcasted_iota(jnp.int32, (tq, LANES), 1) < HEAD_DIM
        q2 = jnp.concatenate([jnp.where(lo_half, q, 0.0), jnp.where(lo_half, 0.0, q)], axis=0)
        aux = jnp.broadcast_to(bq_ref[0:1, :], (2 * tq, LANES))
        qc_s[...] = jnp.concatenate([q2, aux], axis=1).astype(BF16)
        m_s[...] = jnp.full(m_s.shape, NEG, F32)
        l_s[...] = jnp.zeros(l_s.shape, F32)
        acc_s[...] = jnp.zeros(acc_s.shape, F32)

    kc = jnp.concatenate([k_ref[...].astype(BF16), kaux_ref[...]], axis=1)
    v = v_ref[...].astype(BF16)
    is_last = last_tab[step] == 1

    @pl.when(jnp.logical_not(is_last))
    def _interior():
        _flash_update(qc_s, kc, v, m_s, l_s, acc_s)

    @pl.when(is_last)
    def _diag():
        kpos = ki * tk + lax.broadcasted_iota(jnp.int32, (1, tk), 1)
        qpos = qi * tq + lax.broadcasted_iota(jnp.int32, (tq, 1), 0)
        _flash_update(qc_s, kc, v, m_s, l_s, acc_s, causal=kpos <= qpos)
        o = acc_s[...] / l_s[...]
        lam = _lambda_value(lam_ref[...], lam_init)
        o_ref[...] = _subln(o[:tq] - lam * o[tq:], subln_ref[...], lam_init).astype(o_ref.dtype)


def _diff_prompt(qd, kvd, kaux, lam_p, subln, lam_init, tq, tk):
    B, S, _ = qd.shape
    qi, ki, last = _causal_pairs(S, tq, tk)
    bq = jnp.repeat(_bias_q_rows([_diff_slope(h) for h in range(DIFF_HEADS)])[:, None, :], 8, axis=1)
    grid_spec = pltpu.PrefetchScalarGridSpec(
        num_scalar_prefetch=3,
        grid=(B, DIFF_HEADS, int(qi.shape[0])),
        in_specs=[
            pl.BlockSpec((None, tq, DIFF_V_DIM), lambda b, h, s, qt, kt, lt: (b, qt[s], h)),
            pl.BlockSpec((None, tk, DIFF_V_DIM), lambda b, h, s, qt, kt, lt: (b, kt[s], h)),
            pl.BlockSpec((None, tk, DIFF_V_DIM), lambda b, h, s, qt, kt, lt: (b, kt[s], DIFF_HEADS + h)),
            pl.BlockSpec((tk, LANES), lambda b, h, s, qt, kt, lt: (kt[s], 0)),
            pl.BlockSpec((None, 8, LANES), lambda b, h, s, qt, kt, lt: (h, 0, 0)),
            pl.BlockSpec(lam_p.shape, lambda b, h, s, qt, kt, lt: (0, 0)),
            pl.BlockSpec(subln.shape, lambda b, h, s, qt, kt, lt: (0, 0)),
        ],
        out_specs=pl.BlockSpec((None, tq, DIFF_V_DIM), lambda b, h, s, qt, kt, lt: (b, qt[s], h)),
        scratch_shapes=[
            pltpu.VMEM((2 * tq, 2 * LANES), BF16),
            pltpu.VMEM((2 * tq, LANES), F32),
            pltpu.VMEM((2 * tq, LANES), F32),
            pltpu.VMEM((2 * tq, DIFF_V_DIM), F32),
        ],
    )
    return pl.pallas_call(
        functools.partial(_diff_prompt_kernel, tq=tq, tk=tk, lam_init=lam_init),
        grid_spec=grid_spec,
        out_shape=jax.ShapeDtypeStruct((B, S, DIFF_WIDTH), BF16),
        compiler_params=_cparams(("parallel", "parallel", "arbitrary")),
    )(qi, ki, last, qd, kvd, kvd, kaux, bq, lam_p, subln)


def _merge_kernel(x_ref, oa_ref, ob_ref, gm_ref, wa_ref, wb_ref, wo_ref, h_ref):
    d = x_ref.shape[-1]
    ya = jnp.dot(oa_ref[...].astype(BF16), wa_ref[...], preferred_element_type=F32)
    yb = jnp.dot(ob_ref[...].astype(BF16), wb_ref[...], preferred_element_type=F32)
    y = gm_ref[:, :d] * ya + gm_ref[:, d:] * yb
    h_ref[...] = x_ref[...] + jnp.dot(y.astype(BF16), wo_ref[...], preferred_element_type=F32)


def _merge(x2d, oa, ob, gm, wa, wb, wo, tm):
    T, D = x2d.shape
    row = lambda w: pl.BlockSpec((tm, w), lambda i: (i, 0))
    full = lambda a: pl.BlockSpec(a.shape, lambda i: (0, 0))
    return pl.pallas_call(
        _merge_kernel,
        grid=(T // tm,),
        in_specs=[row(D), row(NSA_WIDTH), row(DIFF_WIDTH), row(2 * D), full(wa), full(wb), full(wo)],
        out_specs=row(D),
        out_shape=jax.ShapeDtypeStruct((T, D), F32),
        compiler_params=_cparams(("parallel",)),
    )(x2d, oa, ob, gm, wa, wb, wo)


def _ffn_kernel(*refs, tiles_per_seq, per_row_state):
    if per_row_state:
        (h_ref, gn_ref, wa_ref, wg_ref, cwa_ref, cwg_ref, cba_ref, cbg_ref, wd_ref, pa_ref, pg_ref,
         y_ref, ua_o, ug_o, xn_s, acc_s) = refs
    else:
        (h_ref, gn_ref, wa_ref, wg_ref, cwa_ref, cwg_ref, cba_ref, cbg_ref, wd_ref,
         y_ref, ua_o, ug_o, xn_s, acc_s, carry_s) = refs
    i, j = pl.program_id(0), pl.program_id(1)
    tm = h_ref.shape[0]

    @pl.when(j == 0)
    def _init():
        x = h_ref[...]
        ms = jnp.mean(x * x, axis=-1, keepdims=True)
        xn_s[...] = (x * lax.rsqrt(ms + RMS_EPS) * gn_ref[...]).astype(BF16)
        acc_s[...] = jnp.zeros(acc_s.shape, F32)

    if not per_row_state:
        @pl.when((i % tiles_per_seq) == 0)
        def _zero_state():
            carry_s[j] = jnp.zeros(carry_s.shape[1:], F32)

    xn = xn_s[...]
    row = lax.broadcasted_iota(jnp.int32, (8, 1), 0)

    def conv(w_ref, cw_ref, cb_ref, prev_ref, part, u_out):
        u = jnp.dot(xn, w_ref[...], preferred_element_type=F32)
        cw = cw_ref[...]
        if per_row_state:
            prev = prev_ref[...]
            p2, p1 = prev[:, 0, :], prev[:, 1, :]
            u_out[:, 0, :] = p1
            u_out[:, 1, :] = u
        else:
            carry = carry_s[j, part]
            r1, r2 = pltpu.roll(u, 1, 0), pltpu.roll(u, 2, 0)
            h1 = jnp.where(row == 0, carry[1:2, :], r1[:8])
            h2 = jnp.where(row == 0, carry[0:1, :], jnp.where(row == 1, carry[1:2, :], r2[:8]))
            p1 = jnp.concatenate([h1, r1[8:]], axis=0)
            p2 = jnp.concatenate([h2, r2[8:]], axis=0)
            tail = u[tm - 2:, :]
            carry_s[j, part] = tail
            u_out[...] = tail
        return cb_ref[...] + cw[2:3, :] * u + cw[1:2, :] * p1 + cw[0:1, :] * p2

    a = conv(wa_ref, cwa_ref, cba_ref, None if not per_row_state else pa_ref, 0, ua_o)
    g = conv(wg_ref, cwg_ref, cbg_ref, None if not per_row_state else pg_ref, 1, ug_o)
    act = (g * jax.nn.sigmoid(g) * a).astype(BF16)
    acc_s[...] += jnp.dot(act, wd_ref[...], preferred_element_type=F32)

    @pl.when(j == pl.num_programs(1) - 1)
    def _fin():
        y_ref[...] = h_ref[...] + acc_s[...]


def _ffn(h2d, gn, w_up, conv_w, conv_b, w_down, tm, fc, seq_len=None, state=None):
    T, D = h2d.shape
    dff = w_down.shape[0]
    nf = dff // fc
    per_row = state is not None
    cb2 = conv_b[None, :]
    in_specs = [
        pl.BlockSpec((tm, D), lambda i, j: (i, 0)),
        pl.BlockSpec(gn.shape, lambda i, j: (0, 0)),
        pl.BlockSpec((D, fc), lambda i, j: (0, j)),
        pl.BlockSpec((D, fc), lambda i, j: (0, nf + j)),
        pl.BlockSpec((CONV_W, fc), lambda i, j: (0, j)),
        pl.BlockSpec((CONV_W, fc), lambda i, j: (0, nf + j)),
        pl.BlockSpec((1, fc), lambda i, j: (0, j)),
        pl.BlockSpec((1, fc), lambda i, j: (0, nf + j)),
        pl.BlockSpec((fc, D), lambda i, j: (j, 0)),
    ]
    args = [h2d, gn, w_up, w_up, conv_w, conv_w, cb2, cb2, w_down]
    scratch = [pltpu.VMEM((tm, D), BF16), pltpu.VMEM((tm, D), F32)]
    if per_row:
        in_specs += [pl.BlockSpec((tm, CONV_W - 1, fc), lambda i, j: (i, 0, j)),
                     pl.BlockSpec((tm, CONV_W - 1, fc), lambda i, j: (i, 0, nf + j))]
        args += [state, state]
        n_state, tiles_per_seq = T, 1
        st_spec = pl.BlockSpec((tm, CONV_W - 1, fc), lambda i, j: (i, 0, j))
    else:
        tiles_per_seq = seq_len // tm
        n_state = T // tm
        scratch.append(pltpu.VMEM((nf, 2, CONV_W - 1, fc), F32))
        st_spec = pl.BlockSpec((None, CONV_W - 1, fc), lambda i, j: (i, 0, j))
    y, ua, ug = pl.pallas_call(
        functools.partial(_ffn_kernel, tiles_per_seq=tiles_per_seq, per_row_state=per_row),
        grid=(T // tm, nf),
        in_specs=in_specs,
        out_specs=[pl.BlockSpec((tm, D), lambda i, j: (i, 0)), st_spec, st_spec],
        out_shape=[jax.ShapeDtypeStruct((T, D), F32),
                   jax.ShapeDtypeStruct((n_state, CONV_W - 1, dff), F32),
                   jax.ShapeDtypeStruct((n_state, CONV_W - 1, dff), F32)],
        scratch_shapes=scratch,
        compiler_params=_cparams(("arbitrary", "arbitrary")),
    )(*args)
    state_out = jnp.concatenate([ua, ug], axis=-1)
    if not per_row:
        state_out = state_out[tiles_per_seq - 1::tiles_per_seq]
    return y, state_out


def _rows1(q_row, scale):
    return _to_rows(q_row, scale)


def _slope_col(fn, n, rep):
    return jnp.concatenate([jnp.full((1, 1), fn(h // rep), F32) for h in range(n * rep)], axis=0)


def _nsa_local_sample_kernel(q_ref, gate_ref, cmp_ref, win_ref, kvw_ref, loc_o, mask_o, list_o,
                             *, past, w_buf, n_pages, n_list):
    nc = (past + 1) // NSA_BLOCK
    ns = -(-(past + 1) // NSA_BLOCK)
    n_sel = min(NSA_TOPK, ns)
    nsp = -(-ns // LANES) * LANES
    bpp = PAGE_SIZE // NSA_BLOCK
    q2 = _rows1(q_ref[...], ATTN_SCALE)
    q2b = q2.astype(BF16)
    gate = gate_ref[...]
    slope = _slope_col(_nsa_slope, NSA_HEADS, 1)

    cmp = cmp_ref[...][:nc]
    kc, vc = cmp[:, :LANES], cmp[:, LANES:]
    bend = lax.broadcasted_iota(jnp.int32, (1, nc), 1) * NSA_BLOCK + (NSA_BLOCK - 1)
    dist_c = past - bend
    valid_c = dist_c >= 0
    p_c = _softmax_rows(_nt(q2, kc, precision=HIGHEST) - slope * dist_c.astype(F32), valid_c)
    o_c = _nn(p_c, vc, precision=HIGHEST)

    lane = lax.broadcasted_iota(jnp.int32, (1, nsp), 1)
    cur = past // NSA_BLOCK
    forced = (lane == 0) | (lane == cur) | (lane == cur - 1)
    ii = lax.broadcasted_iota(jnp.int32, (nsp, nsp), 0)
    jj = lax.broadcasted_iota(jnp.int32, (nsp, nsp), 1)
    pg = lax.broadcasted_iota(jnp.int32, (n_pages, nsp), 0)
    nn = lax.broadcasted_iota(jnp.int32, (n_pages, nsp), 1)
    tok_lo = lax.broadcasted_iota(jnp.int32, (n_pages, PAGE_SIZE), 1) < NSA_BLOCK
    masks, need = [], jnp.zeros((n_pages, 1), F32)
    for g in range(NSA_KV_HEADS):
        imp = jnp.sum(p_c[g * NSA_GROUP:(g + 1) * NSA_GROUP], axis=0, keepdims=True)
        imp = jnp.concatenate([imp, jnp.zeros((1, nsp - nc), F32)], axis=1)
        score = jnp.where(forced, FORCE_SCORE, jnp.where(lane > cur, -1.0, imp))
        score = jnp.where(lane < ns, score, -3e38)
        col = jnp.sum(jnp.where(ii == jj, score, 0.0), axis=1, keepdims=True)
        ahead = (col > score) | ((col == score) & (ii < jj))
        cnt = jnp.sum(ahead.astype(jnp.int32), axis=0, keepdims=True)
        sel = ((cnt < n_sel) & (lane < ns)).astype(F32)
        c0 = jnp.sum(jnp.where(nn == pg * bpp, sel, 0.0), axis=1, keepdims=True)
        c1 = jnp.sum(jnp.where(nn == pg * bpp + 1, sel, 0.0), axis=1, keepdims=True)
        masks.append(jnp.where(tok_lo, c0, c1))
        need = need + c0 + c1

    pi = lax.broadcasted_iota(jnp.int32, (n_pages, n_pages), 0)
    pj = lax.broadcasted_iota(jnp.int32, (n_pages, n_pages), 1)
    needed = (need > 0.5).astype(F32)
    need_row = jnp.sum(jnp.where(pi == pj, needed, 0.0), axis=0, keepdims=True)
    before = jnp.sum(jnp.where(pi < pj, needed, 0.0), axis=0, keepdims=True)
    slot = lax.broadcasted_iota(jnp.int32, (n_list, n_pages), 0).astype(F32)
    onehot = (slot == before) & (need_row > 0.5)
    page = lax.broadcasted_iota(jnp.int32, (n_list, n_pages), 1).astype(F32)
    list_o[...] = jnp.sum(jnp.where(onehot, page, 0.0), axis=1, keepdims=True).astype(jnp.int32)
    pick = jnp.where(onehot, 1.0, 0.0).astype(BF16)
    for g in range(NSA_KV_HEADS):
        mask_o[g] = jnp.dot(pick, masks[g].astype(BF16), preferred_element_type=F32)

    win = win_ref[...]
    kw_t, vw_t = win[:LANES].astype(BF16), win[LANES:].astype(BF16)
    kpos = (past - w_buf) + lax.broadcasted_iota(jnp.int32, (1, w_buf), 1)
    dist_w = past - kpos
    valid_w = (dist_w >= 0) & (dist_w < NSA_WINDOW) & (kpos >= 0)
    s_w = jnp.where(valid_w, _nn(q2b, kw_t) - slope * dist_w.astype(F32), NEG)
    new = kvw_ref[...]
    kn, vn = new[:, :LANES], new[:, LANES:]
    s_n = jnp.sum(q2 * kn, axis=-1, keepdims=True)
    m = jnp.maximum(jnp.max(s_w, axis=-1, keepdims=True), s_n)
    p_w = jnp.where(valid_w, jnp.exp(s_w - m), 0.0)
    p_n = jnp.exp(s_n - m)
    o_w = (_nt(p_w.astype(BF16), vw_t) + p_n * vn) / (jnp.sum(p_w, axis=-1, keepdims=True) + p_n)

    loc_o[...] = _gate_rows(gate, 0, 1) * o_c + _gate_rows(gate, 2, 1) * o_w


def _nsa_local_sample(q, gate, kvcmp, win_state, kvw_new, layer, past, n_list):
    DB = q.shape[0]
    nc_all = kvcmp.shape[1]
    w_buf = win_state.shape[3]
    n_pages = past // PAGE_SIZE
    return pl.pallas_call(
        functools.partial(_nsa_local_sample_kernel, past=past, w_buf=w_buf, n_pages=n_pages, n_list=n_list),
        grid=(DB,),
        in_specs=[
            pl.BlockSpec((None, 1, NSA_WIDTH), lambda b: (b, 0, 0)),
            pl.BlockSpec((None, 1, GATE_PAD), lambda b: (b, 0, 0)),
            pl.BlockSpec((None, nc_all, KV_W), lambda b: (b, 0, 0)),
            pl.BlockSpec((None, None, KV_W, w_buf), lambda b: (layer, b, 0, 0)),
            pl.BlockSpec((None, 1, KV_W), lambda b: (b, 0, 0)),
        ],
        out_specs=[pl.BlockSpec((None, NSA_HEADS, LANES), lambda b: (b, 0, 0)),
                   pl.BlockSpec((None, NSA_KV_HEADS, n_list, PAGE_SIZE), lambda b: (b, 0, 0, 0)),
                   pl.BlockSpec((None, n_list, 1), lambda b: (b, 0, 0))],
        out_shape=[jax.ShapeDtypeStruct((DB, NSA_HEADS, LANES), F32),
                   jax.ShapeDtypeStruct((DB, NSA_KV_HEADS, n_list, PAGE_SIZE), F32),
                   jax.ShapeDtypeStruct((DB, n_list, 1), jnp.int32)],
        compiler_params=_cparams(("parallel",)),
    )(q, gate, kvcmp, win_state, kvw_new)


def _nsa_sel_sample_kernel(*refs, ppc, past):
    list_ref = refs[1]
    pages = refs[2:2 + ppc]
    q_ref, gate_ref, new_ref, mask_ref, loc_ref, o_ref, q2_s, m_s, l_s, acc_s = refs[2 + ppc:]
    b, c = pl.program_id(0), pl.program_id(1)
    slope = _slope_col(_nsa_slope, NSA_HEADS, 1)

    @pl.when(c == 0)
    def _init():
        q2_s[...] = _rows1(q_ref[...], ATTN_SCALE)
        m_s[...] = jnp.full(m_s.shape, NEG, F32)
        l_s[...] = jnp.zeros(l_s.shape, F32)
        acc_s[...] = jnp.zeros(acc_s.shape, F32)

    q2 = q2_s[...]
    q2b = q2.astype(BF16)
    tiles = [r[...] for r in pages]
    ks_t = jnp.concatenate([t[:LANES] for t in tiles], axis=1).astype(BF16)
    vs_t = jnp.concatenate([t[LANES:] for t in tiles], axis=1).astype(BF16)
    n = ppc * PAGE_SIZE
    tok = lax.broadcasted_iota(jnp.int32, (1, PAGE_SIZE), 1)
    kpos = jnp.concatenate([list_ref[b, c * ppc + i] * PAGE_SIZE + tok for i in range(ppc)], axis=1)
    dist = (past - kpos).astype(F32)
    mask = mask_ref[...]
    vrow = [jnp.concatenate([mask[g, i:i + 1, :] for i in range(ppc)], axis=1) for g in range(NSA_KV_HEADS)]
    second_group = lax.broadcasted_iota(jnp.int32, (NSA_HEADS, n), 0) >= NSA_GROUP
    valid = jnp.where(second_group, vrow[1], vrow[0]) > 0.5
    s = jnp.where(valid, _nn(q2b, ks_t) - slope * dist, NEG)
    m_old = m_s[...]
    m_new = jnp.maximum(m_old, jnp.max(s, axis=-1, keepdims=True))
    alpha = jnp.exp(m_old - m_new)
    p = jnp.where(valid, jnp.exp(s - m_new), 0.0)
    l_s[...] = alpha * l_s[...] + jnp.sum(p, axis=-1, keepdims=True)
    acc_s[...] = alpha * acc_s[...] + _nt(p.astype(BF16), vs_t)
    m_s[...] = m_new

    @pl.when(c == pl.num_programs(1) - 1)
    def _fin():
        new = new_ref[...]
        kn, vn = new[:, :LANES], new[:, LANES:]
        s_n = jnp.sum(q2 * kn, axis=-1, keepdims=True)
        m_old = m_s[...]
        m_new = jnp.maximum(m_old, s_n)
        alpha = jnp.exp(m_old - m_new)
        p_n = jnp.exp(s_n - m_new)
        o_s = (alpha * acc_s[...] + p_n * vn) / (alpha * l_s[...] + p_n)
        o_ref[...] = _from_rows(loc_ref[...] + _gate_rows(gate_ref[...], 1, 1) * o_s, 1)


def _nsa_sel_sample(pool, page_table, page_list, q, gate, kvs_new, mask, loc, layer, past, ppc):
    DB, n_list = page_list.shape

    def page_spec(i):
        return pl.BlockSpec((None, None, KV_W, PAGE_SIZE),
                            lambda b, c, pt, lst: (pt[b, lst[b, c * ppc + i]], layer, 0, 0))

    grid_spec = pltpu.PrefetchScalarGridSpec(
        num_scalar_prefetch=2,
        grid=(DB, n_list // ppc),
        in_specs=[page_spec(i) for i in range(ppc)] + [
            pl.BlockSpec((None, 1, NSA_WIDTH), lambda b, c, pt, lst: (b, 0, 0)),
            pl.BlockSpec((None, 1, GATE_PAD), lambda b, c, pt, lst: (b, 0, 0)),
            pl.BlockSpec((None, 1, KV_W), lambda b, c, pt, lst: (b, 0, 0)),
            pl.BlockSpec((None, NSA_KV_HEADS, ppc, PAGE_SIZE), lambda b, c, pt, lst: (b, 0, c, 0)),
            pl.BlockSpec((None, NSA_HEADS, LANES), lambda b, c, pt, lst: (b, 0, 0)),
        ],
        out_specs=pl.BlockSpec((None, 1, NSA_WIDTH), lambda b, c, pt, lst: (b, 0, 0)),
        scratch_shapes=[pltpu.VMEM((NSA_HEADS, LANES), F32), pltpu.VMEM((NSA_HEADS, 1), F32),
                        pltpu.VMEM((NSA_HEADS, 1), F32), pltpu.VMEM((NSA_HEADS, LANES), F32)],
    )
    return pl.pallas_call(
        functools.partial(_nsa_sel_sample_kernel, ppc=ppc, past=past),
        grid_spec=grid_spec,
        out_shape=jax.ShapeDtypeStruct((DB, 1, NSA_WIDTH), F32),
        compiler_params=_cparams(("parallel", "arbitrary")),
    )(page_table, page_list, *([pool] * ppc), q, gate, kvs_new, mask, loc)


def _diff_sample_kernel(*refs, ppc, past, lam_init):
    pages = refs[1:1 + ppc]
    q_ref, new_ref, lam_ref, subln_ref, o_ref, q2_s, m_s, l_s, acc_s = refs[1 + ppc:]
    c = pl.program_id(1)
    rows = 2 * DIFF_HEADS
    slots = 2 * DIFF_HEADS
    slope = _slope_col(_diff_slope, DIFF_HEADS, 2)

    @pl.when(c == 0)
    def _init():
        q = q_ref[...] * ATTN_SCALE
        lo_half = lax.broadcasted_iota(jnp.int32, (1, LANES), 1) < HEAD_DIM
        pieces = []
        for r in range(rows):
            piece = q[:, (r // 2) * LANES:(r // 2 + 1) * LANES]
            pieces.append(jnp.where(lo_half == (r % 2 == 0), piece, 0.0))
        q2_s[...] = jnp.concatenate(pieces, axis=0)
        m_s[...] = jnp.full(m_s.shape, NEG, F32)
        l_s[...] = jnp.zeros(l_s.shape, F32)
        acc_s[...] = jnp.zeros(acc_s.shape, F32)

    q2 = q2_s[...]
    kv = jnp.concatenate([r[...] for r in pages], axis=0).astype(BF16)
    n = ppc * PAGE_SIZE * slots
    lane = lax.broadcasted_iota(jnp.int32, (rows, n), 1)
    head = lax.broadcasted_iota(jnp.int32, (rows, n), 0) // 2
    valid = (lane % slots) == head
    kpos = c * (ppc * PAGE_SIZE) + lane[:1] // slots
    s = jnp.where(valid, _nt(q2.astype(BF16), kv) - slope * (past - kpos).astype(F32), NEG)
    m_old = m_s[...]
    m_new = jnp.maximum(m_old, jnp.max(s, axis=-1, keepdims=True))
    alpha = jnp.exp(m_old - m_new)
    p = jnp.exp(s - m_new)
    l_s[...] = alpha * l_s[...] + jnp.sum(p, axis=-1, keepdims=True)
    acc_s[...] = alpha * acc_s[...] + _nn(pltpu.roll(p, DIFF_HEADS, 1).astype(BF16), kv)
    m_s[...] = m_new

    @pl.when(c == pl.num_programs(1) - 1)
    def _fin():
        new = new_ref[...]
        kn = jnp.concatenate([new[:, (r // 2) * LANES:(r // 2 + 1) * LANES] for r in range(rows)], axis=0)
        vn = jnp.concatenate([new[:, DIFF_WIDTH + (r // 2) * LANES:DIFF_WIDTH + (r // 2 + 1) * LANES]
                              for r in range(rows)], axis=0)
        s_n = jnp.sum(q2 * kn, axis=-1, keepdims=True)
        m_old = m_s[...]
        m_new = jnp.maximum(m_old, s_n)
        alpha = jnp.exp(m_old - m_new)
        p_n = jnp.exp(s_n - m_new)
        o = (alpha * acc_s[...] + p_n * vn) / (alpha * l_s[...] + p_n)
        lam = _lambda_value(lam_ref[...], lam_init)
        outs = [_subln(o[2 * h:2 * h + 1] - lam * o[2 * h + 1:2 * h + 2], subln_ref[...], lam_init)
                for h in range(DIFF_HEADS)]
        o_ref[...] = jnp.concatenate(outs, axis=1)


def _diff_sample(pool, page_table, qd, kvd_new, lam_p, subln, lam_init, layer, past, ppc):
    DB, n_pages = page_table.shape
    grid_spec = pltpu.PrefetchScalarGridSpec(
        num_scalar_prefetch=1,
        grid=(DB, n_pages // ppc),
        in_specs=_page_specs(ppc, layer, PAGE_SIZE * 2 * DIFF_HEADS, DIFF_V_DIM) + [
            pl.BlockSpec((None, 1, DIFF_WIDTH), lambda b, c, pt: (b, 0, 0)),
            pl.BlockSpec((None, 1, 2 * DIFF_WIDTH), lambda b, c, pt: (b, 0, 0)),
            pl.BlockSpec(lam_p.shape, lambda b, c, pt: (0, 0)),
            pl.BlockSpec(subln.shape, lambda b, c, pt: (0, 0)),
        ],
        out_specs=pl.BlockSpec((None, 1, DIFF_WIDTH), lambda b, c, pt: (b, 0, 0)),
        scratch_shapes=[pltpu.VMEM((2 * DIFF_HEADS, DIFF_V_DIM), F32), pltpu.VMEM((2 * DIFF_HEADS, 1), F32),
                        pltpu.VMEM((2 * DIFF_HEADS, 1), F32), pltpu.VMEM((2 * DIFF_HEADS, DIFF_V_DIM), F32)],
    )
    return pl.pallas_call(
        functools.partial(_diff_sample_kernel, ppc=ppc, past=past, lam_init=lam_init),
        grid_spec=grid_spec,
        out_shape=jax.ShapeDtypeStruct((DB, 1, DIFF_WIDTH), F32),
        compiler_params=_cparams(("parallel", "arbitrary")),
    )(page_table, *([pool] * ppc), qd, kvd_new, lam_p, subln)


def _pick(n, prefs):
    for p in prefs:
        if n % p == 0:
            return p
    return n


def _ffn_chunk(dff):
    for k in (2, 1, 4, 11, 22):
        if dff % k == 0 and (dff // k) % LANES == 0 and dff // k <= 1536:
            return dff // k
    return dff


def kernel(x_prompt, x_sample, cache_nsa_cmp, cache_nsa_slc, cache_diff, state_nsa_win, state_ffn_conv,
           page_table, norm_attn, w_in, qk_gain_nsa, qk_gain_diff, nsa_cmp_w, diff_lambda, diff_subln,
           w_br_a, w_br_b, w_o, norm_ffn, w_up, conv_w, conv_b, w_down):
    B, S, D = x_prompt.shape
    DB, t_new, _ = x_sample.shape
    assert t_new == 1, "the sample group carries one new token per sequence"
    depth = w_in.shape[0]
    n_pages = page_table.shape[1]
    past = n_pages * PAGE_SIZE
    n_phys = cache_nsa_cmp.shape[0]
    w_buf = state_nsa_win.shape[2]
    dff = w_down.shape[1]
    assert S % PAGE_SIZE == 0 and S >= NSA_WINDOW + PAGE_SIZE

    tm_proj = _pick(S, (256, 128))
    tm_row = _pick(S, (512, 256, 128))
    tq_nsa = 128
    tk_nsa = _pick(S, (512, 256, 128))
    tq_diff = _pick(S, (512, 256, 128))
    tk_diff = tq_diff
    fc = _ffn_chunk(dff)
    ppc = _pick(n_pages, (16, 8, 4, 2, 1))
    ppc_cmp = _pick(n_pages, (32, 16, 8, 4))
    ppc_sel = 8
    n_sel_s = min(NSA_TOPK, -(-(past + 1) // NSA_BLOCK))
    n_list = min(n_pages, -(-NSA_KV_HEADS * n_sel_s // ppc_sel) * ppc_sel)
    assert n_list % ppc_sel == 0

    chan_major = lambda a: jnp.transpose(a, (0, 1, 3, 4, 5, 2)).reshape(a.shape[0], a.shape[1], KV_W, a.shape[2])
    pool_cmp = chan_major(cache_nsa_cmp)
    pool_slc = chan_major(cache_nsa_slc)
    pool_diff = cache_diff.reshape(n_phys, depth, PAGE_SIZE * 2 * DIFF_HEADS, DIFF_V_DIM)
    win_state = chan_major(state_nsa_win)
    pmat = _avg_matrix()
    kaux = _key_aux(S)

    xp = x_prompt.reshape(B * S, D)
    xs = x_sample.reshape(DB, D)
    outs_p = {k: [] for k in ("cmp", "slc", "diff", "win", "conv")}
    outs_s = {k: [] for k in ("cmp", "slc", "diff", "win", "conv")}
    w_keep_p = min(NSA_WINDOW, S)
    w_keep_s = min(NSA_WINDOW, w_buf + 1)
    kvd_cache = None

    for l in range(depth):
        lam_init = 0.8 - 0.6 * math.exp(-0.3 * l)
        w_packed = _pack_w_in(w_in[l])
        gains = _pack_gains(qk_gain_nsa[l], qk_gain_diff[l])
        gn_a = norm_attn[l][None, :]
        gn_f = norm_ffn[l][None, :]
        wexp = _cmp_weights(nsa_cmp_w[l])
        w8 = _cmp_page_weights(nsa_cmp_w[l])
        wa, wb, wo = w_br_a[l].astype(BF16), w_br_b[l].astype(BF16), w_o[l].astype(BF16)
        wup, wdn = w_up[l].astype(BF16), w_down[l].astype(BF16)
        lam_p = diff_lambda[l].astype(F32)
        subln = diff_subln[l][None, :]

        q, kvc, kvs, kvw, gate, qd, kvd, gm, kvd_cache = _proj(xp, gn_a, w_packed, gains, pmat, tm_proj,
                                                                cache_out=(B, S, depth, l, kvd_cache))
        r3 = lambda a: a.reshape(B, S, a.shape[-1])
        kvcmp = _compress_prompt(r3(kvc), wexp, tk_nsa)
        o_a = _nsa_prompt(r3(q), r3(gate), kvcmp, r3(kvw), r3(kvs), kaux, tq_nsa, tk_nsa)
        o_b = _diff_prompt(r3(qd), r3(kvd), kaux, lam_p, subln, lam_init, tq_diff, tk_diff)
        hp = _merge(xp, o_a.reshape(B * S, -1), o_b.reshape(B * S, -1), gm, wa, wb, wo, tm_row)
        xp, conv_p = _ffn(hp, gn_f, wup, conv_w[l], conv_b[l], wdn, tm_row, fc, seq_len=S)
        outs_p["cmp"].append(kvc.reshape(B, S, 2, NSA_KV_HEADS, HEAD_DIM))
        outs_p["slc"].append(kvs.reshape(B, S, 2, NSA_KV_HEADS, HEAD_DIM))
        outs_p["win"].append(r3(kvw)[:, S - w_keep_p:].reshape(B, w_keep_p, 2, NSA_KV_HEADS, HEAD_DIM))
        outs_p["conv"].append(conv_p)

        q, kvc, kvs, kvw, gate, qd, kvd, gm = _proj(xs, gn_a, w_packed, gains, pmat, DB)
        r1 = lambda a: a.reshape(DB, 1, a.shape[-1])
        kvcmp = _compress_pages(pool_cmp, page_table, w8, l, ppc_cmp)
        loc, mask, plist = _nsa_local_sample(r1(q), r1(gate), kvcmp, win_state, r1(kvw), l, past, n_list)
        o_a = _nsa_sel_sample(pool_slc, page_table, plist.reshape(DB, n_list), r1(q), r1(gate), r1(kvs), mask, loc,
                              l, past, ppc_sel)
        o_b = _diff_sample(pool_diff, page_table, r1(qd), r1(kvd), lam_p, subln, lam_init, l, past, ppc)
        hs = _merge(xs, o_a.reshape(DB, -1), o_b.reshape(DB, -1), gm, wa, wb, wo, DB)
        xs, conv_s = _ffn(hs, gn_f, wup, conv_w[l], conv_b[l], wdn, DB, fc, state=state_ffn_conv[l])
        outs_s["cmp"].append(kvc.reshape(DB, 1, 2, NSA_KV_HEADS, HEAD_DIM))
        outs_s["slc"].append(kvs.reshape(DB, 1, 2, NSA_KV_HEADS, HEAD_DIM))
        outs_s["diff"].append(kvd.reshape(DB, 1, 2, DIFF_HEADS, DIFF_V_DIM))
        win_all = jnp.concatenate([win_state[l], kvw[:, :, None]], axis=2)[:, :, w_buf + 1 - w_keep_s:]
        outs_s["win"].append(jnp.transpose(win_all.reshape(DB, 2, NSA_KV_HEADS, HEAD_DIM, w_keep_s), (0, 4, 1, 2, 3)))
        outs_s["conv"].append(conv_s)

    return (xp.reshape(B, S, D), xs.reshape(DB, 1, D),
            jnp.stack(outs_p["cmp"], axis=1), jnp.stack(outs_p["slc"], axis=1),
            kvd_cache.reshape(B, depth, S, 2, DIFF_HEADS, DIFF_V_DIM),
            jnp.stack(outs_p["win"], axis=0), jnp.stack(outs_p["conv"], axis=0),
            jnp.stack(outs_s["cmp"], axis=1), jnp.stack(outs_s["slc"], axis=1), jnp.stack(outs_s["diff"], axis=1),
            jnp.stack(outs_s["win"], axis=0), jnp.stack(outs_s["conv"], axis=0))
```

```python
import functools
import math

import numpy as np
import jax
import jax.numpy as jnp
from jax import lax
from jax.experimental import pallas as pl
from jax.experimental.pallas import tpu as pltpu

HEAD_DIM = 64
NSA_HEADS = 8
NSA_KV_HEADS = 2
NSA_GROUP = NSA_HEADS // NSA_KV_HEADS
NSA_BLOCK = 64
NSA_TOPK = 16
NSA_WINDOW = 512
NSA_WIDTH = NSA_HEADS * HEAD_DIM
DIFF_HEADS = 4
DIFF_V_DIM = 2 * HEAD_DIM
DIFF_WIDTH = DIFF_HEADS * DIFF_V_DIM
CONV_W = 3
PAGE_SIZE = 128
RMS_EPS = 1e-6
ATTN_SCALE = HEAD_DIM ** -0.5
FORCE_SCORE = 1e4
KV_W = 2 * NSA_KV_HEADS * HEAD_DIM
LANES = 128
GATE_PAD = LANES
NEG = -1e30
VMEM_LIMIT = 56 * 1024 * 1024

F32 = jnp.float32
BF16 = jnp.bfloat16
HIGHEST = lax.Precision.HIGHEST

C_Q = 0
C_KVC = C_Q + NSA_WIDTH
C_KVS = C_KVC + KV_W
C_KVW = C_KVS + KV_W
C_GATE = C_KVW + KV_W
C_QD = C_GATE + GATE_PAD
C_KD = C_QD + DIFF_WIDTH
C_VD = C_KD + DIFF_WIDTH
C_GM = C_VD + DIFF_WIDTH


def _nsa_slope(h):
    g, r = h // NSA_GROUP, h % NSA_GROUP
    return 2.0 ** (-8.0 * (r * NSA_KV_HEADS + g + 1) / NSA_HEADS)


def _diff_slope(h):
    return 2.0 ** (-8.0 * (h + 1) / DIFF_HEADS)


LOG2E = math.log2(math.e)
MASK_BIG = 2.0 ** 100
AUX_BIAS = HEAD_DIM


def _split3(x):
    parts, rest = [], np.float64(x)
    for _ in range(3):
        p = np.float64(np.asarray(rest, dtype=np.float32).astype(jnp.bfloat16).astype(np.float32))
        parts.append(float(p))
        rest = rest - p
    return parts


def _bias_q_rows(slopes):
    rows = np.zeros((len(slopes), LANES), np.float32)
    for i, s in enumerate(slopes):
        rows[i, AUX_BIAS:AUX_BIAS + 6] = _split3(s * LOG2E) * 2
    return jnp.asarray(rows)


def _key_aux(n_keys):
    kpos = np.arange(n_keys)
    aux = np.zeros((n_keys, LANES), np.float32)
    blk = kpos // NSA_BLOCK
    ok = blk < AUX_BIAS
    aux[kpos[ok], blk[ok]] = 1.0
    aux[:, AUX_BIAS:AUX_BIAS + 3] = (NSA_BLOCK * blk)[:, None]
    aux[:, AUX_BIAS + 3:AUX_BIAS + 6] = (kpos % NSA_BLOCK)[:, None]
    return jnp.asarray(aux, dtype=BF16)


FLASH_ROWS = 1024


def _flash_update(qc_s, kc, v, m_s, l_s, acc_s, causal=None, groups=1):
    rows = qc_s.shape[0]
    qc, m_all, l_all, acc_all = qc_s[...], m_s[...], l_s[...], acc_s[...]
    m_out, l_out, acc_out = [], [], []
    chunk = min(FLASH_ROWS, rows)
    for r0 in range(0, rows, chunk):
        rs = slice(r0, r0 + chunk)
        s = _nt(qc[rs, :], kc)
        if causal is not None:
            t, n = causal.shape
            if chunk >= t:
                s = jnp.where(causal[None], s.reshape(chunk // t, t, n), NEG).reshape(chunk, n)
            else:
                s = jnp.where(causal[r0 % t:r0 % t + chunk], s, NEG)
        m_old = m_all[rs, :]
        m_new = jnp.maximum(m_old, jnp.max(s, axis=-1, keepdims=True))
        alpha = jnp.exp2(m_old - m_new)
        p = jnp.exp2((s - jnp.concatenate([m_new] * (s.shape[1] // LANES), axis=1)).astype(BF16))
        l_out.append(alpha * l_all[rs, :] + jnp.sum(p.astype(F32), axis=-1, keepdims=True))
        acc_out.append(alpha * acc_all[rs, :] + _nn(p, v))
        m_out.append(m_new)
    m_s[...] = jnp.concatenate(m_out, axis=0)
    l_s[...] = jnp.concatenate(l_out, axis=0)
    acc_s[...] = jnp.concatenate(acc_out, axis=0)


def _nt(a, b, **kw):
    return lax.dot_general(a, b, (((1,), (1,)), ((), ())), preferred_element_type=F32, **kw)


def _nn(a, b, **kw):
    return lax.dot_general(a, b, (((1,), (0,)), ((), ())), preferred_element_type=F32, **kw)


def _cparams(sem):
    return pltpu.CompilerParams(dimension_semantics=sem, vmem_limit_bytes=VMEM_LIMIT)


def _proj_kernel(*refs, n_carried):
    x_ref, gn_ref, w_ref, gains_ref, pmat_ref = refs[:5]
    q_o, kvc_o, kvs_o, kvw_o, gate_o, qd_o, kvd_o, gm_o = refs[5 + n_carried:13 + n_carried]
    cache_o = refs[13 + n_carried:]
    x = x_ref[...]
    ms = jnp.mean(x * x, axis=-1, keepdims=True)
    xn = (x * lax.rsqrt(ms + RMS_EPS) * gn_ref[...]).astype(BF16)
    pmat = pmat_ref[...]

    def seg(c0, width):
        return jnp.dot(xn, w_ref[:, c0:c0 + width], preferred_element_type=F32)

    def headnorm(z, gain_row):
        msq = jnp.dot((z * z).astype(BF16), pmat, preferred_element_type=F32)
        return z * lax.rsqrt(msq + RMS_EPS) * gains_ref[gain_row:gain_row + 1, :]

    def normed(c0, width, gain_row):
        z = seg(c0, width)
        return jnp.concatenate([headnorm(z[:, c:c + LANES], gain_row) for c in range(0, width, LANES)], axis=1)

    q_o[...] = normed(C_Q, NSA_WIDTH, 0)
    for c0, out, row in ((C_KVC, kvc_o, 1), (C_KVS, kvs_o, 2), (C_KVW, kvw_o, 3)):
        z = seg(c0, KV_W)
        out[...] = jnp.concatenate([headnorm(z[:, :LANES], row), z[:, LANES:]], axis=1)
    gate_o[...] = jax.nn.sigmoid(seg(C_GATE, GATE_PAD))
    qd_o[...] = normed(C_QD, DIFF_WIDTH, 4)
    kd, vd = normed(C_KD, DIFF_WIDTH, 5), seg(C_VD, DIFF_WIDTH)
    kvd_o[:, :DIFF_WIDTH] = kd
    kvd_o[:, DIFF_WIDTH:] = vd
    if cache_o:
        tm = x.shape[0]
        slots = 2 * DIFF_HEADS
        for j in range(DIFF_HEADS):
            cache_o[0][pl.ds(j, tm, stride=slots), :] = kd[:, j * DIFF_V_DIM:(j + 1) * DIFF_V_DIM]
            cache_o[0][pl.ds(DIFF_HEADS + j, tm, stride=slots), :] = vd[:, j * DIFF_V_DIM:(j + 1) * DIFF_V_DIM]
    d_model = x.shape[-1]
    for c in range(0, 2 * d_model, 512):
        gm_o[:, c:c + 512] = jax.nn.sigmoid(seg(C_GM + c, 512)).astype(gm_o.dtype)


def _proj(x2d, gn, w_packed, gains, pmat, tm, cache_out=None):
    T, D = x2d.shape
    widths = (NSA_WIDTH, KV_W, KV_W, KV_W, GATE_PAD, DIFF_WIDTH, 2 * DIFF_WIDTH, 2 * D)
    full = lambda a: pl.BlockSpec(a.shape, lambda i: (0, 0))
    in_specs = [pl.BlockSpec((tm, D), lambda i: (i, 0)), full(gn), full(w_packed), full(gains), full(pmat)]
    args = [x2d, gn, w_packed, gains, pmat]
    out_specs = [pl.BlockSpec((tm, w), lambda i: (i, 0)) for w in widths]
    out_shape = [jax.ShapeDtypeStruct((T, w), F32) for w in widths[:-1]]
    out_shape.append(jax.ShapeDtypeStruct((T, widths[-1]), BF16))
    aliases, n_carried = {}, 0
    if cache_out is not None:
        batch, seq, depth, layer, carried = cache_out
        slots = 2 * DIFF_HEADS
        tiles = seq // tm
        out_specs.append(pl.BlockSpec((None, None, tm * slots, DIFF_V_DIM),
                                      lambda i: (i // tiles, layer, i % tiles, 0)))
        out_shape.append(jax.ShapeDtypeStruct((batch, depth, seq * slots, DIFF_V_DIM), F32))
        if carried is not None:
            in_specs.append(pl.BlockSpec(memory_space=pl.ANY))
            args.append(carried)
            aliases, n_carried = {len(args) - 1: len(out_shape) - 1}, 1
    return pl.pallas_call(
        functools.partial(_proj_kernel, n_carried=n_carried),
        grid=(T // tm,),
        in_specs=in_specs,
        out_specs=out_specs,
        out_shape=out_shape,
        input_output_aliases=aliases,
        compiler_params=_cparams(("parallel",)),
    )(*args)


def _pack_w_in(w_in_l):
    D = w_in_l.shape[0]
    sizes = (NSA_WIDTH, KV_W, KV_W, KV_W, 3 * NSA_HEADS, DIFF_WIDTH, DIFF_WIDTH, DIFF_WIDTH, 2 * D)
    offs = np.cumsum(sizes)[:-1].tolist()
    q, kvc, kvs, kvw, gn, qd, kd, vd, gm = jnp.split(w_in_l, offs, axis=1)
    gn = jnp.pad(gn, ((0, 0), (0, GATE_PAD - gn.shape[1])))
    perm = lambda w: w.reshape(D, 2, DIFF_HEADS, HEAD_DIM).transpose(0, 2, 1, 3).reshape(D, DIFF_WIDTH)
    return jnp.concatenate([q, kvc, kvs, kvw, gn, perm(qd), perm(kd), vd, gm], axis=1).astype(BF16)


def _pack_gains(g_nsa, g_diff):
    two = lambda a, b: jnp.concatenate([a, b])[None, :]
    rows = [two(g_nsa[0], g_nsa[0]), two(g_nsa[1], g_nsa[1]), two(g_nsa[2], g_nsa[2]), two(g_nsa[3], g_nsa[3]),
            two(g_diff[0], g_diff[1]), two(g_diff[2], g_diff[3])]
    rows += [jnp.ones((1, LANES), F32)] * 2
    return jnp.concatenate(rows, axis=0).astype(F32)


def _avg_matrix():
    i = np.arange(LANES)
    return jnp.asarray((i[:, None] // HEAD_DIM == i[None, :] // HEAD_DIM) / HEAD_DIM, dtype=BF16)


def _compress_kernel(*refs, n_in, n_prefetch):
    refs = refs[n_prefetch:]
    w_ref, out_ref = refs[n_in], refs[n_in + 1]
    w = w_ref[...]
    outs = []
    for r in refs[:n_in]:
        x = r[...].reshape(-1, NSA_BLOCK, KV_W)
        outs.append(jnp.sum(x * w[None], axis=1))
    out_ref[...] = jnp.concatenate(outs, axis=0).reshape(out_ref.shape)


def _cmp_weights(cmp_w_l):
    return jnp.concatenate([jnp.broadcast_to(cmp_w_l[0][:, None], (NSA_BLOCK, KV_W // 2)),
                            jnp.broadcast_to(cmp_w_l[1][:, None], (NSA_BLOCK, KV_W // 2))], axis=1).astype(F32)


def _compress_prompt(kvc, wexp, tk):
    B, S, _ = kvc.shape
    nb = tk // NSA_BLOCK
    return pl.pallas_call(
        functools.partial(_compress_kernel, n_in=1, n_prefetch=0),
        grid=(B, S // tk),
        in_specs=[pl.BlockSpec((None, tk, KV_W), lambda b, i: (b, i, 0)),
                  pl.BlockSpec(wexp.shape, lambda b, i: (0, 0))],
        out_specs=pl.BlockSpec((None, nb, KV_W), lambda b, i: (b, i, 0)),
        out_shape=jax.ShapeDtypeStruct((B, S // NSA_BLOCK, KV_W), F32),
        compiler_params=_cparams(("parallel", "parallel")),
    )(kvc, wexp)


def _page_specs(n, layer, rows, cols):
    def mk(i):
        return pl.BlockSpec((None, None, rows, cols), lambda b, c, pt: (pt[b, c * n + i], layer, 0, 0))
    return [mk(i) for i in range(n)]


def _split_bf16(x):
    hi = x.astype(BF16)
    return hi, (x - hi.astype(F32)).astype(BF16)


def _compress_pages_kernel(*refs, ppc):
    pages, w_ref, out_ref = refs[1:1 + ppc], refs[1 + ppc], refs[2 + ppc]
    w_hi, w_lo = _split_bf16(w_ref[...])
    w_both = jnp.concatenate([w_hi, w_lo], axis=0)
    k_cols = lax.broadcasted_iota(jnp.int32, (2, KV_W), 1) < KV_W // 2
    outs = []
    for r in pages:
        hi, lo = _split_bf16(r[...])
        o16 = _nt(w_both, hi)
        o8 = o16[:8] + o16[8:] + _nt(w_hi, lo)
        outs.append(jnp.where(k_cols, o8[0:2], o8[2:4]))
    out_ref[...] = jnp.concatenate(outs, axis=0)


def _cmp_page_weights(cmp_w_l):
    z = jnp.zeros((NSA_BLOCK,), F32)
    rows = [jnp.concatenate([cmp_w_l[0], z]), jnp.concatenate([z, cmp_w_l[0]]),
            jnp.concatenate([cmp_w_l[1], z]), jnp.concatenate([z, cmp_w_l[1]])]
    return jnp.concatenate([jnp.stack(rows), jnp.zeros((4, PAGE_SIZE), F32)], axis=0).astype(F32)


def _compress_pages(pool_t, page_table, w8, layer, ppc):
    DB, n_pages = page_table.shape
    bpp = PAGE_SIZE // NSA_BLOCK
    grid_spec = pltpu.PrefetchScalarGridSpec(
        num_scalar_prefetch=1,
        grid=(DB, n_pages // ppc),
        in_specs=_page_specs(ppc, layer, KV_W, PAGE_SIZE) + [pl.BlockSpec(w8.shape, lambda b, c, pt: (0, 0))],
        out_specs=pl.BlockSpec((None, ppc * bpp, KV_W), lambda b, c, pt: (b, c, 0)),
    )
    return pl.pallas_call(
        functools.partial(_compress_pages_kernel, ppc=ppc),
        grid_spec=grid_spec,
        out_shape=jax.ShapeDtypeStruct((DB, n_pages * bpp, KV_W), F32),
        compiler_params=_cparams(("parallel", "parallel")),
    )(page_table, *([pool_t] * ppc), w8)


def _to_rows(q, scale):
    t = q.shape[0]
    hi_half = lax.broadcasted_iota(jnp.int32, (t, LANES), 1) >= HEAD_DIM
    pieces = []
    for h in range(NSA_HEADS):
        g = h // NSA_GROUP
        c = q[:, (h // 2) * LANES:(h // 2 + 1) * LANES]
        if (h % 2) != g:
            c = pltpu.roll(c, HEAD_DIM, 1)
        keep = hi_half if g == 1 else jnp.logical_not(hi_half)
        pieces.append(jnp.where(keep, c * scale, 0.0))
    return jnp.concatenate(pieces, axis=0)


def _from_rows(o, t):
    lo_half = lax.broadcasted_iota(jnp.int32, (t, LANES), 1) < HEAD_DIM
    chunks = []
    for j in range(NSA_HEADS // 2):
        a = o[(2 * j) * t:(2 * j + 1) * t]
        b = o[(2 * j + 1) * t:(2 * j + 2) * t]
        if j // 2 == 0:
            chunks.append(jnp.where(lo_half, a, pltpu.roll(b, HEAD_DIM, 1)))
        else:
            chunks.append(jnp.where(lo_half, pltpu.roll(a, HEAD_DIM, 1), b))
    return jnp.concatenate(chunks, axis=1)


def _gate_rows(gate, branch, t):
    cols = [jnp.broadcast_to(gate[:, branch * NSA_HEADS + h:branch * NSA_HEADS + h + 1], (t, LANES))
            for h in range(NSA_HEADS)]
    return jnp.concatenate(cols, axis=0)


def _softmax_rows(s, valid):
    m = jnp.max(jnp.where(valid, s, NEG), axis=-1, keepdims=True)
    p = jnp.where(valid, jnp.exp(s - m), 0.0)
    return p / jnp.maximum(jnp.sum(p, axis=-1, keepdims=True), 1e-30)


def _nsa_prompt_kernel(qi_tab, ki_tab, last_tab, q_ref, gate_ref, cmp_ref, kvw_ref, kaux_all_ref, bq_ref,
                       kvs_ref, kaux_ref, o_ref, qc_s, loc_s, g1_s, m_s, l_s, acc_s, *, tq, tk, seq):
    step = pl.program_id(1)
    qi, ki = qi_tab[step], ki_tab[step]
    q0 = qi * tq
    nc = seq // NSA_BLOCK
    ns = -(-seq // NSA_BLOCK)
    n_sel = min(NSA_TOPK, ns)
    wk = NSA_WINDOW + tq

    @pl.when(ki == 0)
    def _init():
        q2 = _to_rows(q_ref[...], ATTN_SCALE)
        gate = gate_ref[...]
        qpos_col = q0 + lax.broadcasted_iota(jnp.int32, (tq, 1), 0)
        qpos_row = q0 + lax.broadcasted_iota(jnp.int32, (1, tq), 1)

        cmp = cmp_ref[...]
        kc, vc = cmp[:, :LANES], cmp[:, LANES:]
        bend_row = lax.broadcasted_iota(jnp.int32, (1, nc), 1) * NSA_BLOCK + (NSA_BLOCK - 1)
        dist_c = qpos_col - bend_row
        valid_c = dist_c >= 0
        s_c = _nt(q2.astype(BF16), kc.astype(BF16))
        vcb = vc.astype(BF16)
        o_c = []
        for h in range(NSA_HEADS):
            sh = s_c[h * tq:(h + 1) * tq] - _nsa_slope(h) * dist_c.astype(F32)
            o_c.append(_nn(_softmax_rows(sh, valid_c).astype(BF16), vcb))
        o_c = jnp.concatenate(o_c, axis=0)

        bend_col = lax.broadcasted_iota(jnp.int32, (nc, 1), 0) * NSA_BLOCK + (NSA_BLOCK - 1)
        dist_t = qpos_row - bend_col
        valid_t = dist_t >= 0
        kc_hi, kc_lo = _split_bf16(kc)
        q2_hi, q2_lo = _split_bf16(q2)
        s_t = _nt(kc_hi, q2_hi) + _nt(kc_lo, q2_hi) + _nt(kc_hi, q2_lo)
        blk = lax.broadcasted_iota(jnp.int32, (ns, tq), 0)
        cur = qpos_row // NSA_BLOCK
        sel_t = []
        for g in range(NSA_KV_HEADS):
            imp = jnp.zeros((nc, tq), F32)
            for r in range(NSA_GROUP):
                h = g * NSA_GROUP + r
                sh = s_t[:, h * tq:(h + 1) * tq] - _nsa_slope(h) * dist_t.astype(F32)
                m = jnp.max(jnp.where(valid_t, sh, NEG), axis=0, keepdims=True)
                p = jnp.where(valid_t, jnp.exp(sh - m), 0.0)
                imp = imp + p / jnp.maximum(jnp.sum(p, axis=0, keepdims=True), 1e-30)
            if ns > nc:
                imp = jnp.concatenate([imp, jnp.zeros((ns - nc, tq), F32)], axis=0)
            forced = (blk == 0) | (blk == cur) | (blk == cur - 1)
            score = jnp.where(forced, FORCE_SCORE, jnp.where(blk > cur, -1.0, imp))
            grp = [score[j:j + 8] for j in range(0, ns, 8)]
            below = lax.broadcasted_iota(jnp.int32, (8, tq), 0)
            cnt = [jnp.zeros((8, tq), jnp.int32) for _ in grp]
            for i in range(ns):
                row = jnp.broadcast_to(score[i:i + 1, :], (8, tq))
                for j, sc in enumerate(grp):
                    if 8 * j > i:
                        ahead = row >= sc
                    elif 8 * j + 7 <= i:
                        ahead = row > sc
                    else:
                        ahead = (row > sc) | ((row == sc) & (below > i - 8 * j))
                    cnt[j] = cnt[j] + ahead.astype(jnp.int32)
            sel_t.append((jnp.concatenate(cnt, axis=0) < n_sel).astype(F32))
        sel = jnp.concatenate(sel_t, axis=0).T

        q2l = (q2 * LOG2E).astype(BF16)
        low = lax.broadcasted_iota(jnp.int32, (tq, LANES), 1) < AUX_BIAS
        pad = jnp.ones((tq, LANES - ns), F32)
        aux_w, aux_s = [], []
        for h in range(NSA_HEADS):
            g = h // NSA_GROUP
            bq = jnp.broadcast_to(bq_ref[h:h + 1, :], (tq, LANES))
            drop = (jnp.concatenate([sel[:, g * ns:(g + 1) * ns], pad], axis=1) - 1.0) * MASK_BIG
            aux_w.append(jnp.where(low, 0.0, bq))
            aux_s.append(jnp.where(low, drop, bq))
        qc_w = jnp.concatenate([q2l, jnp.concatenate(aux_w, axis=0).astype(BF16)], axis=1)
        qc_s[...] = jnp.concatenate([q2l, jnp.concatenate(aux_s, axis=0).astype(BF16)], axis=1)

        start = pl.multiple_of(jnp.clip(q0 - NSA_WINDOW, 0, seq - wk), LANES)
        kvw = kvw_ref[pl.ds(start, wk), :]
        kcw = jnp.concatenate([kvw[:, :LANES].astype(BF16), kaux_all_ref[pl.ds(start, wk), :]], axis=1)
        vw = kvw[:, LANES:].astype(BF16)
        kpos = start + lax.broadcasted_iota(jnp.int32, (1, wk), 1)
        dist_w = qpos_col - kpos
        valid_w = ((dist_w >= 0) & (dist_w < NSA_WINDOW))[None]
        s_w = jnp.where(valid_w, _nt(qc_w, kcw).reshape(NSA_HEADS, tq, wk), NEG).reshape(NSA_HEADS * tq, wk)
        p_w = jnp.exp2(s_w - jnp.max(s_w, axis=-1, keepdims=True))
        o_w = _nn(p_w.astype(BF16), vw) / jnp.sum(p_w, axis=-1, keepdims=True)

        loc_s[...] = _gate_rows(gate, 0, tq) * o_c + _gate_rows(gate, 2, tq) * o_w
        g1_s[...] = _gate_rows(gate, 1, tq)
        m_s[...] = jnp.full(m_s.shape, NEG, F32)
        l_s[...] = jnp.zeros(l_s.shape, F32)
        acc_s[...] = jnp.zeros(acc_s.shape, F32)

    kvs = kvs_ref[...]
    kc_t = jnp.concatenate([kvs[:, :LANES].astype(BF16), kaux_ref[...]], axis=1)
    vs = kvs[:, LANES:].astype(BF16)
    is_last = last_tab[step] == 1

    @pl.when(jnp.logical_not(is_last))
    def _interior():
        _flash_update(qc_s, kc_t, vs, m_s, l_s, acc_s)

    @pl.when(is_last)
    def _diag():
        kpos = ki * tk + lax.broadcasted_iota(jnp.int32, (1, tk), 1)
        qpos = q0 + lax.broadcasted_iota(jnp.int32, (tq, 1), 0)
        _flash_update(qc_s, kc_t, vs, m_s, l_s, acc_s, causal=kpos <= qpos)
        o_ref[...] = _from_rows(loc_s[...] + g1_s[...] * (acc_s[...] / l_s[...]), tq).astype(o_ref.dtype)


def _causal_pairs(seq, tq, tk):
    assert tk % tq == 0, "only the last key tile of a query tile may cross the diagonal"
    qi, ki, last = [], [], []
    for i in range(seq // tq):
        n = (i * tq + tq - 1) // tk + 1
        for j in range(n):
            qi.append(i), ki.append(j), last.append(1 if j == n - 1 else 0)
    mk = lambda a: jnp.asarray(np.asarray(a, dtype=np.int32))
    return mk(qi), mk(ki), mk(last)


def _nsa_prompt(q, gate, kvcmp, kvw, kvs, kaux, tq, tk):
    B, S, _ = q.shape
    nc = S // NSA_BLOCK
    ns = -(-S // NSA_BLOCK)
    assert ns <= AUX_BIAS, "the block one-hot must fit below the ALiBi lanes"
    qi, ki, last = _causal_pairs(S, tq, tk)
    bq = _bias_q_rows([_nsa_slope(h) for h in range(NSA_HEADS)])
    rows = NSA_HEADS * tq
    grid_spec = pltpu.PrefetchScalarGridSpec(
        num_scalar_prefetch=3,
        grid=(B, int(qi.shape[0])),
        in_specs=[
            pl.BlockSpec((None, tq, NSA_WIDTH), lambda b, s, qt, kt, lt: (b, qt[s], 0)),
            pl.BlockSpec((None, tq, GATE_PAD), lambda b, s, qt, kt, lt: (b, qt[s], 0)),
            pl.BlockSpec((None, nc, KV_W), lambda b, s, qt, kt, lt: (b, 0, 0)),
            pl.BlockSpec((None, S, KV_W), lambda b, s, qt, kt, lt: (b, 0, 0)),
            pl.BlockSpec((S, LANES), lambda b, s, qt, kt, lt: (0, 0)),
            pl.BlockSpec(bq.shape, lambda b, s, qt, kt, lt: (0, 0)),
            pl.BlockSpec((None, tk, KV_W), lambda b, s, qt, kt, lt: (b, kt[s], 0)),
            pl.BlockSpec((tk, LANES), lambda b, s, qt, kt, lt: (kt[s], 0)),
        ],
        out_specs=pl.BlockSpec((None, tq, NSA_WIDTH), lambda b, s, qt, kt, lt: (b, qt[s], 0)),
        scratch_shapes=[
            pltpu.VMEM((rows, 2 * LANES), BF16),
            pltpu.VMEM((rows, LANES), F32),
            pltpu.VMEM((rows, LANES), F32),
            pltpu.VMEM((rows, LANES), F32),
            pltpu.VMEM((rows, LANES), F32),
            pltpu.VMEM((rows, LANES), F32),
        ],
    )
    return pl.pallas_call(
        functools.partial(_nsa_prompt_kernel, tq=tq, tk=tk, seq=S),
        grid_spec=grid_spec,
        out_shape=jax.ShapeDtypeStruct((B, S, NSA_WIDTH), BF16),
        compiler_params=_cparams(("parallel", "arbitrary")),
    )(qi, ki, last, q, gate, kvcmp, kvw, kaux, bq, kvs, kaux)


def _lambda_value(lp, lam_init):
    a = jnp.sum(lp[0:1] * lp[1:2], axis=-1, keepdims=True)
    b = jnp.sum(lp[2:3] * lp[3:4], axis=-1, keepdims=True)
    return jnp.exp(a) - jnp.exp(b) + lam_init


def _subln(o, g, lam_init):
    ms = jnp.mean(o * o, axis=-1, keepdims=True)
    return o * lax.rsqrt(ms + RMS_EPS) * g * (1.0 - lam_init)


def _diff_prompt_kernel(qi_tab, ki_tab, last_tab, q_ref, k_ref, v_ref, kaux_ref, bq_ref, lam_ref, subln_ref, o_ref,
                        qc_s, m_s, l_s, acc_s, *, tq, tk, lam_init):
    step = pl.program_id(2)
    qi, ki = qi_tab[step], ki_tab[step]

    @pl.when(ki == 0)
    def _init():
        q = q_ref[...] * (ATTN_SCALE * LOG2E)
        lo_half = lax.broadcasted_iota(jnp.int32, (tq, LANES), 1) < HEAD_DIM
        q2 = jnp.concatenate([jnp.where(lo_half, q, 0.0), jnp.where(lo_half, 0.0, q)], axis=0)
        aux = jnp.broadcast_to(bq_ref[0:1, :], (2 * tq, LANES))
        qc_s[...] = jnp.concatenate([q2, aux], axis=1).astype(BF16)
        m_s[...] = jnp.full(m_s.shape, NEG, F32)
        l_s[...] = jnp.zeros(l_s.shape, F32)
        acc_s[...] = jnp.zeros(acc_s.shape, F32)

    kc = jnp.concatenate([k_ref[...].astype(BF16), kaux_ref[...]], axis=1)
    v = v_ref[...].astype(BF16)
    is_last = last_tab[step] == 1

    @pl.when(jnp.logical_not(is_last))
    def _interior():
        _flash_update(qc_s, kc, v, m_s, l_s, acc_s)

    @pl.when(is_last)
    def _diag():
        kpos = ki * tk + lax.broadcasted_iota(jnp.int32, (1, tk), 1)
        qpos = qi * tq + lax.broadcasted_iota(jnp.int32, (tq, 1), 0)
        _flash_update(qc_s, kc, v, m_s, l_s, acc_s, causal=kpos <= qpos)
        o = acc_s[...] / l_s[...]
        lam = _lambda_value(lam_ref[...], lam_init)
        o_ref[...] = _subln(o[:tq] - lam * o[tq:], subln_ref[...], lam_init).astype(o_ref.dtype)


def _diff_prompt(qd, kvd, kaux, lam_p, subln, lam_init, tq, tk):
    B, S, _ = qd.shape
    qi, ki, last = _causal_pairs(S, tq, tk)
    bq = jnp.repeat(_bias_q_rows([_diff_slope(h) for h in range(DIFF_HEADS)])[:, None, :], 8, axis=1)
    grid_spec = pltpu.PrefetchScalarGridSpec(
        num_scalar_prefetch=3,
        grid=(B, DIFF_HEADS, int(qi.shape[0])),
        in_specs=[
            pl.BlockSpec((None, tq, DIFF_V_DIM), lambda b, h, s, qt, kt, lt: (b, qt[s], h)),
            pl.BlockSpec((None, tk, DIFF_V_DIM), lambda b, h, s, qt, kt, lt: (b, kt[s], h)),
            pl.BlockSpec((None, tk, DIFF_V_DIM), lambda b, h, s, qt, kt, lt: (b, kt[s], DIFF_HEADS + h)),
            pl.BlockSpec((tk, LANES), lambda b, h, s, qt, kt, lt: (kt[s], 0)),
            pl.BlockSpec((None, 8, LANES), lambda b, h, s, qt, kt, lt: (h, 0, 0)),
            pl.BlockSpec(lam_p.shape, lambda b, h, s, qt, kt, lt: (0, 0)),
            pl.BlockSpec(subln.shape, lambda b, h, s, qt, kt, lt: (0, 0)),
        ],
        out_specs=pl.BlockSpec((None, tq, DIFF_V_DIM), lambda b, h, s, qt, kt, lt: (b, qt[s], h)),
        scratch_shapes=[
            pltpu.VMEM((2 * tq, 2 * LANES), BF16),
            pltpu.VMEM((2 * tq, LANES), F32),
            pltpu.VMEM((2 * tq, LANES), F32),
            pltpu.VMEM((2 * tq, DIFF_V_DIM), F32),
        ],
    )
    return pl.pallas_call(
        functools.partial(_diff_prompt_kernel, tq=tq, tk=tk, lam_init=lam_init),
        grid_spec=grid_spec,
        out_shape=jax.ShapeDtypeStruct((B, S, DIFF_WIDTH), BF16),
        compiler_params=_cparams(("parallel", "parallel", "arbitrary")),
    )(qi, ki, last, qd, kvd, kvd, kaux, bq, lam_p, subln)


def _merge_kernel(x_ref, oa_ref, ob_ref, gm_ref, wa_ref, wb_ref, wo_ref, h_ref):
    d = x_ref.shape[-1]
    ya = jnp.dot(oa_ref[...].astype(BF16), wa_ref[...], preferred_element_type=F32)
    yb = jnp.dot(ob_ref[...].astype(BF16), wb_ref[...], preferred_element_type=F32)
    y = gm_ref[:, :d] * ya + gm_ref[:, d:] * yb
    h_ref[...] = x_ref[...] + jnp.dot(y.astype(BF16), wo_ref[...], preferred_element_type=F32)


def _merge(x2d, oa, ob, gm, wa, wb, wo, tm):
    T, D = x2d.shape
    row = lambda w: pl.BlockSpec((tm, w), lambda i: (i, 0))
    full = lambda a: pl.BlockSpec(a.shape, lambda i: (0, 0))
    return pl.pallas_call(
        _merge_kernel,
        grid=(T // tm,),
        in_specs=[row(D), row(NSA_WIDTH), row(DIFF_WIDTH), row(2 * D), full(wa), full(wb), full(wo)],
        out_specs=row(D),
        out_shape=jax.ShapeDtypeStruct((T, D), F32),
        compiler_params=_cparams(("parallel",)),
    )(x2d, oa, ob, gm, wa, wb, wo)


def _ffn_kernel(*refs, tiles_per_seq, per_row_state):
    if per_row_state:
        (h_ref, gn_ref, wa_ref, wg_ref, cwa_ref, cwg_ref, cba_ref, cbg_ref, wd_ref, pa_ref, pg_ref,
         y_ref, ua_o, ug_o, xn_s, acc_s) = refs
    else:
        (h_ref, gn_ref, wa_ref, wg_ref, cwa_ref, cwg_ref, cba_ref, cbg_ref, wd_ref,
         y_ref, ua_o, ug_o, xn_s, acc_s, carry_s) = refs
    i, j = pl.program_id(0), pl.program_id(1)
    tm = h_ref.shape[0]

    @pl.when(j == 0)
    def _init():
        x = h_ref[...]
        ms = jnp.mean(x * x, axis=-1, keepdims=True)
        xn_s[...] = (x * lax.rsqrt(ms + RMS_EPS) * gn_ref[...]).astype(BF16)
        acc_s[...] = jnp.zeros(acc_s.shape, F32)

    if not per_row_state:
        @pl.when((i % tiles_per_seq) == 0)
        def _zero_state():
            carry_s[j] = jnp.zeros(carry_s.shape[1:], F32)

    xn = xn_s[...]
    row = lax.broadcasted_iota(jnp.int32, (8, 1), 0)

    def conv(w_ref, cw_ref, cb_ref, prev_ref, part, u_out):
        u = jnp.dot(xn, w_ref[...], preferred_element_type=F32)
        cw = cw_ref[...]
        if per_row_state:
            prev = prev_ref[...]
            p2, p1 = prev[:, 0, :], prev[:, 1, :]
            u_out[:, 0, :] = p1
            u_out[:, 1, :] = u
        else:
            carry = carry_s[j, part]
            r1, r2 = pltpu.roll(u, 1, 0), pltpu.roll(u, 2, 0)
            h1 = jnp.where(row == 0, carry[1:2, :], r1[:8])
            h2 = jnp.where(row == 0, carry[0:1, :], jnp.where(row == 1, carry[1:2, :], r2[:8]))
            p1 = jnp.concatenate([h1, r1[8:]], axis=0)
            p2 = jnp.concatenate([h2, r2[8:]], axis=0)
            tail = u[tm - 2:, :]
            carry_s[j, part] = tail
            u_out[...] = tail
        return cb_ref[...] + cw[2:3, :] * u + cw[1:2, :] * p1 + cw[0:1, :] * p2

    a = conv(wa_ref, cwa_ref, cba_ref, None if not per_row_state else pa_ref, 0, ua_o)
    g = conv(wg_ref, cwg_ref, cbg_ref, None if not per_row_state else pg_ref, 1, ug_o)
    act = (g * jax.nn.sigmoid(g) * a).astype(BF16)
    acc_s[...] += jnp.dot(act, wd_ref[...], preferred_element_type=F32)

    @pl.when(j == pl.num_programs(1) - 1)
    def _fin():
        y_ref[...] = h_ref[...] + acc_s[...]


def _ffn(h2d, gn, w_up, conv_w, conv_b, w_down, tm, fc, seq_len=None, state=None):
    T, D = h2d.shape
    dff = w_down.shape[0]
    nf = dff // fc
    per_row = state is not None
    cb2 = conv_b[None, :]
    in_specs = [
        pl.BlockSpec((tm, D), lambda i, j: (i, 0)),
        pl.BlockSpec(gn.shape, lambda i, j: (0, 0)),
        pl.BlockSpec((D, fc), lambda i, j: (0, j)),
        pl.BlockSpec((D, fc), lambda i, j: (0, nf + j)),
        pl.BlockSpec((CONV_W, fc), lambda i, j: (0, j)),
        pl.BlockSpec((CONV_W, fc), lambda i, j: (0, nf + j)),
        pl.BlockSpec((1, fc), lambda i, j: (0, j)),
        pl.BlockSpec((1, fc), lambda i, j: (0, nf + j)),
        pl.BlockSpec((fc, D), lambda i, j: (j, 0)),
    ]
    args = [h2d, gn, w_up, w_up, conv_w, conv_w, cb2, cb2, w_down]
    scratch = [pltpu.VMEM((tm, D), BF16), pltpu.VMEM((tm, D), F32)]
    if per_row:
        in_specs += [pl.BlockSpec((tm, CONV_W - 1, fc), lambda i, j: (i, 0, j)),
                     pl.BlockSpec((tm, CONV_W - 1, fc), lambda i, j: (i, 0, nf + j))]
        args += [state, state]
        n_state, tiles_per_seq = T, 1
        st_spec = pl.BlockSpec((tm, CONV_W - 1, fc), lambda i, j: (i, 0, j))
    else:
        tiles_per_seq = seq_len // tm
        n_state = T // tm
        scratch.append(pltpu.VMEM((nf, 2, CONV_W - 1, fc), F32))
        st_spec = pl.BlockSpec((None, CONV_W - 1, fc), lambda i, j: (i, 0, j))
    y, ua, ug = pl.pallas_call(
        functools.partial(_ffn_kernel, tiles_per_seq=tiles_per_seq, per_row_state=per_row),
        grid=(T // tm, nf),
        in_specs=in_specs,
        out_specs=[pl.BlockSpec((tm, D), lambda i, j: (i, 0)), st_spec, st_spec],
        out_shape=[jax.ShapeDtypeStruct((T, D), F32),
                   jax.ShapeDtypeStruct((n_state, CONV_W - 1, dff), F32),
                   jax.ShapeDtypeStruct((n_state, CONV_W - 1, dff), F32)],
        scratch_shapes=scratch,
        compiler_params=_cparams(("arbitrary", "arbitrary")),
    )(*args)
    state_out = jnp.concatenate([ua, ug], axis=-1)
    if not per_row:
        state_out = state_out[tiles_per_seq - 1::tiles_per_seq]
    return y, state_out


def _rows1(q_row, scale):
    return _to_rows(q_row, scale)


def _slope_col(fn, n, rep):
    return jnp.concatenate([jnp.full((1, 1), fn(h // rep), F32) for h in range(n * rep)], axis=0)


def _nsa_local_sample_kernel(q_ref, gate_ref, cmp_ref, win_ref, kvw_ref, loc_o, mask_o, list_o,
                             *, past, w_buf, n_pages, n_list):
    nc = (past + 1) // NSA_BLOCK
    ns = -(-(past + 1) // NSA_BLOCK)
    n_sel = min(NSA_TOPK, ns)
    nsp = -(-ns // LANES) * LANES
    bpp = PAGE_SIZE // NSA_BLOCK
    q2 = _rows1(q_ref[...], ATTN_SCALE)
    q2b = q2.astype(BF16)
    gate = gate_ref[...]
    slope = _slope_col(_nsa_slope, NSA_HEADS, 1)

    cmp = cmp_ref[...][:nc]
    kc, vc = cmp[:, :LANES], cmp[:, LANES:]
    bend = lax.broadcasted_iota(jnp.int32, (1, nc), 1) * NSA_BLOCK + (NSA_BLOCK - 1)
    dist_c = past - bend
    valid_c = dist_c >= 0
    p_c = _softmax_rows(_nt(q2, kc, precision=HIGHEST) - slope * dist_c.astype(F32), valid_c)
    o_c = _nn(p_c, vc, precision=HIGHEST)

    lane = lax.broadcasted_iota(jnp.int32, (1, nsp), 1)
    cur = past // NSA_BLOCK
    forced = (lane == 0) | (lane == cur) | (lane == cur - 1)
    ii = lax.broadcasted_iota(jnp.int32, (nsp, nsp), 0)
    jj = lax.broadcasted_iota(jnp.int32, (nsp, nsp), 1)
    pg = lax.broadcasted_iota(jnp.int32, (n_pages, nsp), 0)
    nn = lax.broadcasted_iota(jnp.int32, (n_pages, nsp), 1)
    tok_lo = lax.broadcasted_iota(jnp.int32, (n_pages, PAGE_SIZE), 1) < NSA_BLOCK
    masks, need = [], jnp.zeros((n_pages, 1), F32)
    for g in range(NSA_KV_HEADS):
        imp = jnp.sum(p_c[g * NSA_GROUP:(g + 1) * NSA_GROUP], axis=0, keepdims=True)
        imp = jnp.concatenate([imp, jnp.zeros((1, nsp - nc), F32)], axis=1)
        score = jnp.where(forced, FORCE_SCORE, jnp.where(lane > cur, -1.0, imp))
        score = jnp.where(lane < ns, score, -3e38)
        col = jnp.sum(jnp.where(ii == jj, score, 0.0), axis=1, keepdims=True)
        ahead = (col > score) | ((col == score) & (ii < jj))
        cnt = jnp.sum(ahead.astype(jnp.int32), axis=0, keepdims=True)
        sel = ((cnt < n_sel) & (lane < ns)).astype(F32)
        c0 = jnp.sum(jnp.where(nn == pg * bpp, sel, 0.0), axis=1, keepdims=True)
        c1 = jnp.sum(jnp.where(nn == pg * bpp + 1, sel, 0.0), axis=1, keepdims=True)
        masks.append(jnp.where(tok_lo, c0, c1))
        need = need + c0 + c1

    pi = lax.broadcasted_iota(jnp.int32, (n_pages, n_pages), 0)
    pj = lax.broadcasted_iota(jnp.int32, (n_pages, n_pages), 1)
    needed = (need > 0.5).astype(F32)
    need_row = jnp.sum(jnp.where(pi == pj, needed, 0.0), axis=0, keepdims=True)
    before = jnp.sum(jnp.where(pi < pj, needed, 0.0), axis=0, keepdims=True)
    slot = lax.broadcasted_iota(jnp.int32, (n_list, n_pages), 0).astype(F32)
    onehot = (slot == before) & (need_row > 0.5)
    page = lax.broadcasted_iota(jnp.int32, (n_list, n_pages), 1).astype(F32)
    list_o[...] = jnp.sum(jnp.where(onehot, page, 0.0), axis=1, keepdims=True).astype(jnp.int32)
    pick = jnp.where(onehot, 1.0, 0.0).astype(BF16)
    for g in range(NSA_KV_HEADS):
        mask_o[g] = jnp.dot(pick, masks[g].astype(BF16), preferred_element_type=F32)

    win = win_ref[...]
    kw_t, vw_t = win[:LANES].astype(BF16), win[LANES:].astype(BF16)
    kpos = (past - w_buf) + lax.broadcasted_iota(jnp.int32, (1, w_buf), 1)
    dist_w = past - kpos
    valid_w = (dist_w >= 0) & (dist_w < NSA_WINDOW) & (kpos >= 0)
    s_w = jnp.where(valid_w, _nn(q2b, kw_t) - slope * dist_w.astype(F32), NEG)
    new = kvw_ref[...]
    kn, vn = new[:, :LANES], new[:, LANES:]
    s_n = jnp.sum(q2 * kn, axis=-1, keepdims=True)
    m = jnp.maximum(jnp.max(s_w, axis=-1, keepdims=True), s_n)
    p_w = jnp.where(valid_w, jnp.exp(s_w - m), 0.0)
    p_n = jnp.exp(s_n - m)
    o_w = (_nt(p_w.astype(BF16), vw_t) + p_n * vn) / (jnp.sum(p_w, axis=-1, keepdims=True) + p_n)

    loc_o[...] = _gate_rows(gate, 0, 1) * o_c + _gate_rows(gate, 2, 1) * o_w


def _nsa_local_sample(q, gate, kvcmp, win_state, kvw_new, layer, past, n_list):
    DB = q.shape[0]
    nc_all = kvcmp.shape[1]
    w_buf = win_state.shape[3]
    n_pages = past // PAGE_SIZE
    return pl.pallas_call(
        functools.partial(_nsa_local_sample_kernel, past=past, w_buf=w_buf, n_pages=n_pages, n_list=n_list),
        grid=(DB,),
        in_specs=[
            pl.BlockSpec((None, 1, NSA_WIDTH), lambda b: (b, 0, 0)),
            pl.BlockSpec((None, 1, GATE_PAD), lambda b: (b, 0, 0)),
            pl.BlockSpec((None, nc_all, KV_W), lambda b: (b, 0, 0)),
            pl.BlockSpec((None, None, KV_W, w_buf), lambda b: (layer, b, 0, 0)),
            pl.BlockSpec((None, 1, KV_W), lambda b: (b, 0, 0)),
        ],
        out_specs=[pl.BlockSpec((None, NSA_HEADS, LANES), lambda b: (b, 0, 0)),
                   pl.BlockSpec((None, NSA_KV_HEADS, n_list, PAGE_SIZE), lambda b: (b, 0, 0, 0)),
                   pl.BlockSpec((None, n_list, 1), lambda b: (b, 0, 0))],
        out_shape=[jax.ShapeDtypeStruct((DB, NSA_HEADS, LANES), F32),
                   jax.ShapeDtypeStruct((DB, NSA_KV_HEADS, n_list, PAGE_SIZE), F32),
                   jax.ShapeDtypeStruct((DB, n_list, 1), jnp.int32)],
        compiler_params=_cparams(("parallel",)),
    )(q, gate, kvcmp, win_state, kvw_new)


def _nsa_sel_sample_kernel(*refs, ppc, past):
    list_ref = refs[1]
    pages = refs[2:2 + ppc]
    q_ref, gate_ref, new_ref, mask_ref, loc_ref, o_ref, q2_s, m_s, l_s, acc_s = refs[2 + ppc:]
    b, c = pl.program_id(0), pl.program_id(1)
    slope = _slope_col(_nsa_slope, NSA_HEADS, 1)

    @pl.when(c == 0)
    def _init():
        q2_s[...] = _rows1(q_ref[...], ATTN_SCALE)
        m_s[...] = jnp.full(m_s.shape, NEG, F32)
        l_s[...] = jnp.zeros(l_s.shape, F32)
        acc_s[...] = jnp.zeros(acc_s.shape, F32)

    q2 = q2_s[...]
    q2b = q2.astype(BF16)
    tiles = [r[...] for r in pages]
    ks_t = jnp.concatenate([t[:LANES] for t in tiles], axis=1).astype(BF16)
    vs_t = jnp.concatenate([t[LANES:] for t in tiles], axis=1).astype(BF16)
    n = ppc * PAGE_SIZE
    tok = lax.broadcasted_iota(jnp.int32, (1, PAGE_SIZE), 1)
    kpos = jnp.concatenate([list_ref[b, c * ppc + i] * PAGE_SIZE + tok for i in range(ppc)], axis=1)
    dist = (past - kpos).astype(F32)
    mask = mask_ref[...]
    vrow = [jnp.concatenate([mask[g, i:i + 1, :] for i in range(ppc)], axis=1) for g in range(NSA_KV_HEADS)]
    second_group = lax.broadcasted_iota(jnp.int32, (NSA_HEADS, n), 0) >= NSA_GROUP
    valid = jnp.where(second_group, vrow[1], vrow[0]) > 0.5
    s = jnp.where(valid, _nn(q2b, ks_t) - slope * dist, NEG)
    m_old = m_s[...]
    m_new = jnp.maximum(m_old, jnp.max(s, axis=-1, keepdims=True))
    alpha = jnp.exp(m_old - m_new)
    p = jnp.where(valid, jnp.exp(s - m_new), 0.0)
    l_s[...] = alpha * l_s[...] + jnp.sum(p, axis=-1, keepdims=True)
    acc_s[...] = alpha * acc_s[...] + _nt(p.astype(BF16), vs_t)
    m_s[...] = m_new

    @pl.when(c == pl.num_programs(1) - 1)
    def _fin():
        new = new_ref[...]
        kn, vn = new[:, :LANES], new[:, LANES:]
        s_n = jnp.sum(q2 * kn, axis=-1, keepdims=True)
        m_old = m_s[...]
        m_new = jnp.maximum(m_old, s_n)
        alpha = jnp.exp(m_old - m_new)
        p_n = jnp.exp(s_n - m_new)
        o_s = (alpha * acc_s[...] + p_n * vn) / (alpha * l_s[...] + p_n)
        o_ref[...] = _from_rows(loc_ref[...] + _gate_rows(gate_ref[...], 1, 1) * o_s, 1)


def _nsa_sel_sample(pool, page_table, page_list, q, gate, kvs_new, mask, loc, layer, past, ppc):
    DB, n_list = page_list.shape

    def page_spec(i):
        return pl.BlockSpec((None, None, KV_W, PAGE_SIZE),
                            lambda b, c, pt, lst: (pt[b, lst[b, c * ppc + i]], layer, 0, 0))

    grid_spec = pltpu.PrefetchScalarGridSpec(
        num_scalar_prefetch=2,
        grid=(DB, n_list // ppc),
        in_specs=[page_spec(i) for i in range(ppc)] + [
            pl.BlockSpec((None, 1, NSA_WIDTH), lambda b, c, pt, lst: (b, 0, 0)),
            pl.BlockSpec((None, 1, GATE_PAD), lambda b, c, pt, lst: (b, 0, 0)),
            pl.BlockSpec((None, 1, KV_W), lambda b, c, pt, lst: (b, 0, 0)),
            pl.BlockSpec((None, NSA_KV_HEADS, ppc, PAGE_SIZE), lambda b, c, pt, lst: (b, 0, c, 0)),
            pl.BlockSpec((None, NSA_HEADS, LANES), lambda b, c, pt, lst: (b, 0, 0)),
        ],
        out_specs=pl.BlockSpec((None, 1, NSA_WIDTH), lambda b, c, pt, lst: (b, 0, 0)),
        scratch_shapes=[pltpu.VMEM((NSA_HEADS, LANES), F32), pltpu.VMEM((NSA_HEADS, 1), F32),
                        pltpu.VMEM((NSA_HEADS, 1), F32), pltpu.VMEM((NSA_HEADS, LANES), F32)],
    )
    return pl.pallas_call(
        functools.partial(_nsa_sel_sample_kernel, ppc=ppc, past=past),
        grid_spec=grid_spec,
        out_shape=jax.ShapeDtypeStruct((DB, 1, NSA_WIDTH), F32),
        compiler_params=_cparams(("parallel", "arbitrary")),
    )(page_table, page_list, *([pool] * ppc), q, gate, kvs_new, mask, loc)


def _diff_sample_kernel(*refs, ppc, past, lam_init):
    pages = refs[1:1 + ppc]
    q_ref, new_ref, lam_ref, subln_ref, o_ref, q2_s, m_s, l_s, acc_s = refs[1 + ppc:]
    c = pl.program_id(1)
    rows = 2 * DIFF_HEADS
    slots = 2 * DIFF_HEADS
    slope = _slope_col(_diff_slope, DIFF_HEADS, 2)

    @pl.when(c == 0)
    def _init():
        q = q_ref[...] * ATTN_SCALE
        lo_half = lax.broadcasted_iota(jnp.int32, (1, LANES), 1) < HEAD_DIM
        pieces = []
        for r in range(rows):
            piece = q[:, (r // 2) * LANES:(r // 2 + 1) * LANES]
            pieces.append(jnp.where(lo_half == (r % 2 == 0), piece, 0.0))
        q2_s[...] = jnp.concatenate(pieces, axis=0)
        m_s[...] = jnp.full(m_s.shape, NEG, F32)
        l_s[...] = jnp.zeros(l_s.shape, F32)
        acc_s[...] = jnp.zeros(acc_s.shape, F32)

    q2 = q2_s[...]
    kv = jnp.concatenate([r[...] for r in pages], axis=0).astype(BF16)
    n = ppc * PAGE_SIZE * slots
    lane = lax.broadcasted_iota(jnp.int32, (rows, n), 1)
    head = lax.broadcasted_iota(jnp.int32, (rows, n), 0) // 2
    valid = (lane % slots) == head
    kpos = c * (ppc * PAGE_SIZE) + lane[:1] // slots
    s = jnp.where(valid, _nt(q2.astype(BF16), kv) - slope * (past - kpos).astype(F32), NEG)
    m_old = m_s[...]
    m_new = jnp.maximum(m_old, jnp.max(s, axis=-1, keepdims=True))
    alpha = jnp.exp(m_old - m_new)
    p = jnp.exp(s - m_new)
    l_s[...] = alpha * l_s[...] + jnp.sum(p, axis=-1, keepdims=True)
    acc_s[...] = alpha * acc_s[...] + _nn(pltpu.roll(p, DIFF_HEADS, 1).astype(BF16), kv)
    m_s[...] = m_new

    @pl.when(c == pl.num_programs(1) - 1)
    def _fin():
        new = new_ref[...]
        kn = jnp.concatenate([new[:, (r // 2) * LANES:(r // 2 + 1) * LANES] for r in range(rows)], axis=0)
        vn = jnp.concatenate([new[:, DIFF_WIDTH + (r // 2) * LANES:DIFF_WIDTH + (r // 2 + 1) * LANES]
                              for r in range(rows)], axis=0)
        s_n = jnp.sum(q2 * kn, axis=-1, keepdims=True)
        m_old = m_s[...]
        m_new = jnp.maximum(m_old, s_n)
        alpha = jnp.exp(m_old - m_new)
        p_n = jnp.exp(s_n - m_new)
        o = (alpha * acc_s[...] + p_n * vn) / (alpha * l_s[...] + p_n)
        lam = _lambda_value(lam_ref[...], lam_init)
        outs = [_subln(o[2 * h:2 * h + 1] - lam * o[2 * h + 1:2 * h + 2], subln_ref[...], lam_init)
                for h in range(DIFF_HEADS)]
        o_ref[...] = jnp.concatenate(outs, axis=1)


def _diff_sample(pool, page_table, qd, kvd_new, lam_p, subln, lam_init, layer, past, ppc):
    DB, n_pages = page_table.shape
    grid_spec = pltpu.PrefetchScalarGridSpec(
        num_scalar_prefetch=1,
        grid=(DB, n_pages // ppc),
        in_specs=_page_specs(ppc, layer, PAGE_SIZE * 2 * DIFF_HEADS, DIFF_V_DIM) + [
            pl.BlockSpec((None, 1, DIFF_WIDTH), lambda b, c, pt: (b, 0, 0)),
            pl.BlockSpec((None, 1, 2 * DIFF_WIDTH), lambda b, c, pt: (b, 0, 0)),
            pl.BlockSpec(lam_p.shape, lambda b, c, pt: (0, 0)),
            pl.BlockSpec(subln.shape, lambda b, c, pt: (0, 0)),
        ],
        out_specs=pl.BlockSpec((None, 1, DIFF_WIDTH), lambda b, c, pt: (b, 0, 0)),
        scratch_shapes=[pltpu.VMEM((2 * DIFF_HEADS, DIFF_V_DIM), F32), pltpu.VMEM((2 * DIFF_HEADS, 1), F32),
                        pltpu.VMEM((2 * DIFF_HEADS, 1), F32), pltpu.VMEM((2 * DIFF_HEADS, DIFF_V_DIM), F32)],
    )
    return pl.pallas_call(
        functools.partial(_diff_sample_kernel, ppc=ppc, past=past, lam_init=lam_init),
        grid_spec=grid_spec,
        out_shape=jax.ShapeDtypeStruct((DB, 1, DIFF_WIDTH), F32),
        compiler_params=_cparams(("parallel", "arbitrary")),
    )(page_table, *([pool] * ppc), qd, kvd_new, lam_p, subln)


def _pick(n, prefs):
    for p in prefs:
        if n % p == 0:
            return p
    return n


def _ffn_chunk(dff):
    for k in (2, 1, 4, 11, 22):
        if dff % k == 0 and (dff // k) % LANES == 0 and dff // k <= 1536:
            return dff // k
    return dff


def kernel(x_prompt, x_sample, cache_nsa_cmp, cache_nsa_slc, cache_diff, state_nsa_win, state_ffn_conv,
           page_table, norm_attn, w_in, qk_gain_nsa, qk_gain_diff, nsa_cmp_w, diff_lambda, diff_subln,
           w_br_a, w_br_b, w_o, norm_ffn, w_up, conv_w, conv_b, w_down):
    B, S, D = x_prompt.shape
    DB, t_new, _ = x_sample.shape
    assert t_new == 1, "the sample group carries one new token per sequence"
    depth = w_in.shape[0]
    n_pages = page_table.shape[1]
    past = n_pages * PAGE_SIZE
    n_phys = cache_nsa_cmp.shape[0]
    w_buf = state_nsa_win.shape[2]
    dff = w_down.shape[1]
    assert S % PAGE_SIZE == 0 and S >= NSA_WINDOW + PAGE_SIZE

    tm_proj = _pick(S, (256, 128))
    tm_row = _pick(S, (512, 256, 128))
    tq_nsa = 128
    tk_nsa = _pick(S, (512, 256, 128))
    tq_diff = _pick(S, (512, 256, 128))
    tk_diff = tq_diff
    fc = _ffn_chunk(dff)
    ppc = _pick(n_pages, (16, 8, 4, 2, 1))
    ppc_cmp = _pick(n_pages, (32, 16, 8, 4))
    ppc_sel = 8
    n_sel_s = min(NSA_TOPK, -(-(past + 1) // NSA_BLOCK))
    n_list = min(n_pages, -(-NSA_KV_HEADS * n_sel_s // ppc_sel) * ppc_sel)
    assert n_list % ppc_sel == 0

    chan_major = lambda a: jnp.transpose(a, (0, 1, 3, 4, 5, 2)).reshape(a.shape[0], a.shape[1], KV_W, a.shape[2])
    pool_cmp = chan_major(cache_nsa_cmp)
    pool_slc = chan_major(cache_nsa_slc)
    pool_diff = cache_diff.reshape(n_phys, depth, PAGE_SIZE * 2 * DIFF_HEADS, DIFF_V_DIM)
    win_state = chan_major(state_nsa_win)
    pmat = _avg_matrix()
    kaux = _key_aux(S)

    xp = x_prompt.reshape(B * S, D)
    xs = x_sample.reshape(DB, D)
    outs_p = {k: [] for k in ("cmp", "slc", "diff", "win", "conv")}
    outs_s = {k: [] for k in ("cmp", "slc", "diff", "win", "conv")}
    w_keep_p = min(NSA_WINDOW, S)
    w_keep_s = min(NSA_WINDOW, w_buf + 1)
    kvd_cache = None

    for l in range(depth):
        lam_init = 0.8 - 0.6 * math.exp(-0.3 * l)
        w_packed = _pack_w_in(w_in[l])
        gains = _pack_gains(qk_gain_nsa[l], qk_gain_diff[l])
        gn_a = norm_attn[l][None, :]
        gn_f = norm_ffn[l][None, :]
        wexp = _cmp_weights(nsa_cmp_w[l])
        w8 = _cmp_page_weights(nsa_cmp_w[l])
        wa, wb, wo = w_br_a[l].astype(BF16), w_br_b[l].astype(BF16), w_o[l].astype(BF16)
        wup, wdn = w_up[l].astype(BF16), w_down[l].astype(BF16)
        lam_p = diff_lambda[l].astype(F32)
        subln = diff_subln[l][None, :]

        q, kvc, kvs, kvw, gate, qd, kvd, gm, kvd_cache = _proj(xp, gn_a, w_packed, gains, pmat, tm_proj,
                                                                cache_out=(B, S, depth, l, kvd_cache))
        r3 = lambda a: a.reshape(B, S, a.shape[-1])
        kvcmp = _compress_prompt(r3(kvc), wexp, tk_nsa)
        o_a = _nsa_prompt(r3(q), r3(gate), kvcmp, r3(kvw), r3(kvs), kaux, tq_nsa, tk_nsa)
        o_b = _diff_prompt(r3(qd), r3(kvd), kaux, lam_p, subln, lam_init, tq_diff, tk_diff)
        hp = _merge(xp, o_a.reshape(B * S, -1), o_b.reshape(B * S, -1), gm, wa, wb, wo, tm_row)
        xp, conv_p = _ffn(hp, gn_f, wup, conv_w[l], conv_b[l], wdn, tm_row, fc, seq_len=S)
        outs_p["cmp"].append(kvc.reshape(B, S, 2, NSA_KV_HEADS, HEAD_DIM))
        outs_p["slc"].append(kvs.reshape(B, S, 2, NSA_KV_HEADS, HEAD_DIM))
        outs_p["win"].append(r3(kvw)[:, S - w_keep_p:].reshape(B, w_keep_p, 2, NSA_KV_HEADS, HEAD_DIM))
        outs_p["conv"].append(conv_p)

        q, kvc, kvs, kvw, gate, qd, kvd, gm = _proj(xs, gn_a, w_packed, gains, pmat, DB)
        r1 = lambda a: a.reshape(DB, 1, a.shape[-1])
        kvcmp = _compress_pages(pool_cmp, page_table, w8, l, ppc_cmp)
        loc, mask, plist = _nsa_local_sample(r1(q), r1(gate), kvcmp, win_state, r1(kvw), l, past, n_list)
        o_a = _nsa_sel_sample(pool_slc, page_table, plist.reshape(DB, n_list), r1(q), r1(gate), r1(kvs), mask, loc,
                              l, past, ppc_sel)
        o_b = _diff_sample(pool_diff, page_table, r1(qd), r1(kvd), lam_p, subln, lam_init, l, past, ppc)
        hs = _merge(xs, o_a.reshape(DB, -1), o_b.reshape(DB, -1), gm, wa, wb, wo, DB)
        xs, conv_s = _ffn(hs, gn_f, wup, conv_w[l], conv_b[l], wdn, DB, fc, state=state_ffn_conv[l])
        outs_s["cmp"].append(kvc.reshape(DB, 1, 2, NSA_KV_HEADS, HEAD_DIM))
        outs_s["slc"].append(kvs.reshape(DB, 1, 2, NSA_KV_HEADS, HEAD_DIM))
        outs_s["diff"].append(kvd.reshape(DB, 1, 2, DIFF_HEADS, DIFF_V_DIM))
        win_all = jnp.concatenate([win_state[l], kvw[:, :, None]], axis=2)[:, :, w_buf + 1 - w_keep_s:]
        outs_s["win"].append(jnp.transpose(win_all.reshape(DB, 2, NSA_KV_HEADS, HEAD_DIM, w_keep_s), (0, 4, 1, 2, 3)))
        outs_s["conv"].append(conv_s)

    return (xp.reshape(B, S, D), xs.reshape(DB, 1, D),
            jnp.stack(outs_p["cmp"], axis=1), jnp.stack(outs_p["slc"], axis=1),
            kvd_cache.reshape(B, depth, S, 2, DIFF_HEADS, DIFF_V_DIM),
            jnp.stack(outs_p["win"], axis=0), jnp.stack(outs_p["conv"], axis=0),
            jnp.stack(outs_s["cmp"], axis=1), jnp.stack(outs_s["slc"], axis=1), jnp.stack(outs_s["diff"], axis=1),
            jnp.stack(outs_s["win"], axis=0), jnp.stack(outs_s["conv"], axis=0))
```
